```python
import math
import jax, jax.numpy as jnp
from jax import lax
import numpy as np

D_MODEL = 1024
BATCH = 4
SEQ = 4096
DEPTH = 1
DEC_BATCH = 128
DEC_SEQ = 8
PAST_LEN = 8192
PAGE_SIZE = 128

HEAD_DIM = 64
N_HEADS_SB = 8
N_HEADS_RW = 8
WIDTH_SB = N_HEADS_SB * HEAD_DIM
WIDTH_RW = N_HEADS_RW * HEAD_DIM
DECAY_LORA = 64
AAA_LORA = 64
GATE_LORA = 160
D_FF = -(-8 * D_MODEL // (3 * 256)) * 256
Q_BLOCK = 128
RMS_EPS = 1e-6
GN_EPS = 64e-5
SB_BIAS_INIT = -8.0
ATT_COLS = 3 * WIDTH_SB
RW_COLS = 3 * WIDTH_RW + DECAY_LORA + AAA_LORA + GATE_LORA
GATE_COLS = 2 * D_MODEL
IN_COLS = ATT_COLS + RW_COLS + GATE_COLS

kernel_name = "stickbreak_rwkv7_gated_hybrid_step"


def rmsnorm(x, g):
    xf = x.astype(jnp.float32)
    y = xf * lax.rsqrt(jnp.mean(xf * xf, axis=-1, keepdims=True) + RMS_EPS)
    return (y * g.astype(jnp.float32)).astype(x.dtype)


def mixer_inputs(x, norm_g, w_in):
    h = rmsnorm(x, norm_g)
    p = h @ w_in
    b, t, _ = x.shape
    q = p[..., 0:WIDTH_SB].reshape(b, t, N_HEADS_SB, HEAD_DIM)
    k = p[..., WIDTH_SB:2 * WIDTH_SB].reshape(b, t, N_HEADS_SB, HEAD_DIM)
    v = p[..., 2 * WIDTH_SB:ATT_COLS].reshape(b, t, N_HEADS_SB, HEAD_DIM)
    rw = p[..., ATT_COLS:ATT_COLS + RW_COLS]
    gate_logits = p[..., ATT_COLS + RW_COLS:]
    return q, k, v, rw, gate_logits


def stick_breaking(q, k, v, q_pos, k_pos, bias):
    z = jnp.einsum('bqhd,bkhd->bhqk', q.astype(jnp.float32), k.astype(jnp.float32)) * (HEAD_DIM ** -0.5)
    z = z + bias.astype(jnp.float32)[None, :, None, None]
    mask = (k_pos[None, :] < q_pos[:, None])[None, None]
    sp = jnp.where(mask, jax.nn.softplus(z), 0.0)
    tail = lax.cumsum(sp, axis=3, reverse=True)
    att = jnp.exp(jnp.where(mask, z - tail, -jnp.inf))
    out = jnp.einsum('bhqk,bkhd->bqhd', att, v.astype(jnp.float32))
    return out.astype(q.dtype)


def stick_breaking_prompt(q, k, v, bias):
    b, t, h, d = q.shape
    n_blk = t // Q_BLOCK
    qb = q.reshape(b, n_blk, Q_BLOCK, h, d).transpose(1, 0, 2, 3, 4)
    pos = jnp.arange(t, dtype=jnp.int32)
    pb = pos.reshape(n_blk, Q_BLOCK)
    out = lax.map(lambda a: stick_breaking(a[0], k, v, a[1], pos, bias), (qb, pb))
    return out.transpose(1, 0, 2, 3, 4).reshape(b, t, h * d)


def stick_breaking_sample(q, k_new, v_new, cache_k_l, cache_v_l, page_table, bias):
    n_seq, t, h, d = q.shape
    past = page_table.shape[1] * cache_k_l.shape[1]
    q_pos = past + jnp.arange(t, dtype=jnp.int32)
    k_pos = jnp.arange(past + t, dtype=jnp.int32)

    def one_seq(args):
        q_s, k_s, v_s, pages = args
        k_past = jnp.take(cache_k_l, pages, axis=0).reshape(past, h, d)
        v_past = jnp.take(cache_v_l, pages, axis=0).reshape(past, h, d)
        k_all = jnp.concatenate([k_past, k_s.astype(k_past.dtype)], axis=0)
        v_all = jnp.concatenate([v_past, v_s.astype(v_past.dtype)], axis=0)
        return stick_breaking(q_s[None], k_all[None], v_all[None], q_pos, k_pos, bias)[0]

    out = lax.map(one_seq, (q, k_new, v_new, page_table))
    return out.reshape(n_seq, t, h * d)


def wkv_scan(s0, r, w, k, v, a, b):
    xs = tuple(u.transpose(1, 0, 2, 3) for u in (r, w, k, v, a, b))

    def step(s, inp):
        r_t, w_t, k_t, v_t, a_t, b_t = inp
        sa = jnp.einsum('bhij,bhj->bhi', s, a_t)
        s = s * w_t[:, :, None, :] + sa[..., None] * b_t[:, :, None, :] + v_t[..., None] * k_t[:, :, None, :]
        y = jnp.einsum('bhij,bhj->bhi', s, r_t)
        return s, y

    s_last, ys = lax.scan(step, s0.astype(jnp.float32), xs)
    return ys.transpose(1, 0, 2, 3), s_last


def rwkv_branch(rw_raw, prev_row, s0, mu, w0, w_decay_up, a0, w_a_up, w_g_up, k_k, k_a, r_k, lnx_w, lnx_b):
    b, t, _ = rw_raw.shape
    f32 = jnp.float32
    rw = rw_raw.astype(f32)
    prev = jnp.concatenate([prev_row[:, None].astype(f32), rw[:, :-1]], axis=1)
    xs = rw + (prev - rw) * mu.astype(f32)
    o = 0
    r = xs[..., o:o + WIDTH_RW]; o += WIDTH_RW
    k = xs[..., o:o + WIDTH_RW]; o += WIDTH_RW
    v = xs[..., o:o + WIDTH_RW]; o += WIDTH_RW
    wd = xs[..., o:o + DECAY_LORA]; o += DECAY_LORA
    ad = xs[..., o:o + AAA_LORA]; o += AAA_LORA
    gd = xs[..., o:o + GATE_LORA]
    w_log = -jax.nn.softplus(-(w0.astype(f32) + jnp.tanh(wd) @ w_decay_up.astype(f32))) - 0.5
    decay = jnp.exp(-jnp.exp(w_log))
    a = jax.nn.sigmoid(a0.astype(f32) + ad @ w_a_up.astype(f32))
    g = jax.nn.sigmoid(gd) @ w_g_up.astype(f32)
    heads = lambda u: u.reshape(b, t, N_HEADS_RW, HEAD_DIM)
    kk = heads(k * k_k.astype(f32))
    kk = kk / jnp.maximum(jnp.sqrt(jnp.sum(kk * kk, axis=-1, keepdims=True)), 1e-12)
    k = k * (1.0 + (a - 1.0) * k_a.astype(f32))
    rh, kh, vh, wh, ah = heads(r), heads(k), heads(v), heads(decay), heads(a)
    y, s_last = wkv_scan(s0, rh, wh, kh, vh, -kk, kk * ah)
    mean = jnp.mean(y, axis=-1, keepdims=True)
    var = jnp.mean(jnp.square(y - mean), axis=-1, keepdims=True)
    yn = ((y - mean) * lax.rsqrt(var + GN_EPS)).reshape(b, t, WIDTH_RW) * lnx_w.astype(f32) + lnx_b.astype(f32)
    bonus = (jnp.sum(rh * kh * r_k.astype(f32), axis=-1, keepdims=True) * vh).reshape(b, t, WIDTH_RW)
    out = (yn + bonus) * g
    return out, s_last, rw_raw[:, -1]


def merge_and_ffn(x, o_sb, o_rw, gate_logits, b_gate, w_up_sb, w_up_rw, w_o, norm_ffn_g, w_ffn_in, w_ffn_out):
    gates = jax.nn.sigmoid((gate_logits + b_gate).astype(jnp.float32)).astype(x.dtype)
    g_sb, g_rw = gates[..., :D_MODEL], gates[..., D_MODEL:]
    merged = g_sb * (o_sb.astype(x.dtype) @ w_up_sb) + g_rw * (o_rw.astype(x.dtype) @ w_up_rw)
    x = x + merged @ w_o
    h = rmsnorm(x, norm_ffn_g)
    gu = h @ w_ffn_in
    x = x + (jax.nn.silu(gu[..., :D_FF]) * gu[..., D_FF:]) @ w_ffn_out
    return x


def setup_inputs(seed: int = 0) -> dict:
    key = jax.random.key(seed)
    ks = jax.random.split(key, 32)
    f32 = jnp.float32
    n_pages = PAST_LEN // PAGE_SIZE
    n_phys = (DEC_BATCH * n_pages * 5) // 4
    nrm = lambda k, shape, s: jax.random.normal(k, shape, f32) * s
    L = DEPTH
    return {
        "x_prompt": nrm(ks[0], (BATCH, SEQ, D_MODEL), 1.0),
        "x_sample": nrm(ks[1], (DEC_BATCH, DEC_SEQ, D_MODEL), 1.0),
        "cache_k": nrm(ks[2], (L, n_phys, PAGE_SIZE, N_HEADS_SB, HEAD_DIM), 1.0),
        "cache_v": nrm(ks[3], (L, n_phys, PAGE_SIZE, N_HEADS_SB, HEAD_DIM), 1.0),
        "page_table": jax.random.permutation(ks[4], n_phys)[:DEC_BATCH * n_pages].reshape(DEC_BATCH, n_pages).astype(jnp.int32),
        "state_wkv": nrm(ks[5], (L, DEC_BATCH, N_HEADS_RW, HEAD_DIM, HEAD_DIM), 0.3),
        "state_shift": nrm(ks[6], (L, DEC_BATCH, RW_COLS), 1.0),
        "norm_mix_g": 1.0 + nrm(ks[7], (L, D_MODEL), 0.05),
        "w_in": nrm(ks[8], (L, D_MODEL, IN_COLS), D_MODEL ** -0.5),
        "sb_bias": SB_BIAS_INIT + nrm(ks[28], (L, N_HEADS_SB), 0.5),
        "b_gate": nrm(ks[9], (L, GATE_COLS), 0.01),
        "mu_shift": jax.random.uniform(ks[10], (L, RW_COLS), f32),
        "w0": -6.0 + 5.0 * jax.random.uniform(ks[11], (L, WIDTH_RW), f32),
        "w_decay_up": nrm(ks[12], (L, DECAY_LORA, WIDTH_RW), 0.1 * DECAY_LORA ** -0.5),
        "a0": nrm(ks[13], (L, WIDTH_RW), 0.1),
        "w_a_up": nrm(ks[14], (L, AAA_LORA, WIDTH_RW), 0.5 * AAA_LORA ** -0.5),
        "w_g_up": nrm(ks[15], (L, GATE_LORA, WIDTH_RW), GATE_LORA ** -0.5),
        "k_k": 0.85 + nrm(ks[16], (L, WIDTH_RW), 0.05),
        "k_a": 1.0 + nrm(ks[17], (L, WIDTH_RW), 0.05),
        "r_k": nrm(ks[18], (L, N_HEADS_RW, HEAD_DIM), 0.1),
        "lnx_w": 1.0 + nrm(ks[19], (L, WIDTH_RW), 0.05),
        "lnx_b": nrm(ks[20], (L, WIDTH_RW), 0.01),
        "w_up_sb": nrm(ks[21], (L, WIDTH_SB, D_MODEL), WIDTH_SB ** -0.5),
        "w_up_rw": nrm(ks[22], (L, WIDTH_RW, D_MODEL), WIDTH_RW ** -0.5),
        "w_o": nrm(ks[23], (L, D_MODEL, D_MODEL), D_MODEL ** -0.5),
        "norm_ffn_g": 1.0 + nrm(ks[24], (L, D_MODEL), 0.05),
        "w_ffn_in": nrm(ks[25], (L, D_MODEL, 2 * D_FF), D_MODEL ** -0.5),
        "w_ffn_out": nrm(ks[26], (L, D_FF, D_MODEL), D_FF ** -0.5),
        "norm_final_g": 1.0 + nrm(ks[27], (D_MODEL,), 0.05),
    }


def reference(x_prompt, x_sample, cache_k, cache_v, page_table, state_wkv, state_shift,
              norm_mix_g, w_in, sb_bias, b_gate, mu_shift, w0, w_decay_up, a0, w_a_up, w_g_up, k_k, k_a, r_k,
              lnx_w, lnx_b, w_up_sb, w_up_rw, w_o, norm_ffn_g, w_ffn_in, w_ffn_out, norm_final_g):
    xp, xs = x_prompt, x_sample
    bp = xp.shape[0]
    kp_l, vp_l, wkvp_l, shp_l = [], [], [], []
    ks_l, vs_l, wkvs_l, shs_l = [], [], [], []
    for l in range(DEPTH):
        rw_params = (mu_shift[l], w0[l], w_decay_up[l], a0[l], w_a_up[l], w_g_up[l],
                     k_k[l], k_a[l], r_k[l], lnx_w[l], lnx_b[l])
        merge_params = (b_gate[l], w_up_sb[l], w_up_rw[l], w_o[l], norm_ffn_g[l], w_ffn_in[l], w_ffn_out[l])
        q, k, v, rw, gl = mixer_inputs(xp, norm_mix_g[l], w_in[l])
        o_sb = stick_breaking_prompt(q, k, v, sb_bias[l])
        zero_shift = jnp.zeros((bp, RW_COLS), jnp.float32)
        zero_wkv = jnp.zeros((bp, N_HEADS_RW, HEAD_DIM, HEAD_DIM), jnp.float32)
        o_rw, wkv_p, shift_p = rwkv_branch(rw, zero_shift, zero_wkv, *rw_params)
        xp = merge_and_ffn(xp, o_sb, o_rw, gl, *merge_params)
        kp_l.append(k); vp_l.append(v); wkvp_l.append(wkv_p); shp_l.append(shift_p)
        q, k, v, rw, gl = mixer_inputs(xs, norm_mix_g[l], w_in[l])
        o_sb = stick_breaking_sample(q, k, v, cache_k[l], cache_v[l], page_table, sb_bias[l])
        o_rw, wkv_s, shift_s = rwkv_branch(rw, state_shift[l], state_wkv[l], *rw_params)
        xs = merge_and_ffn(xs, o_sb, o_rw, gl, *merge_params)
        ks_l.append(k); vs_l.append(v); wkvs_l.append(wkv_s); shs_l.append(shift_s)
    y_prompt = rmsnorm(xp, norm_final_g)
    y_sample = rmsnorm(xs, norm_final_g)
    k_prompt = jnp.stack(kp_l); v_prompt = jnp.stack(vp_l)
    wkv_prompt = jnp.stack(wkvp_l); shift_prompt = jnp.stack(shp_l)
    k_sample = jnp.stack(ks_l); v_sample = jnp.stack(vs_l)
    wkv_sample = jnp.stack(wkvs_l); shift_sample = jnp.stack(shs_l)
    return (y_prompt, y_sample, k_prompt, v_prompt, wkv_prompt, shift_prompt, k_sample, v_sample, wkv_sample, shift_sample)
```

```python
import functools

import jax
import jax.numpy as jnp
from jax import lax
from jax.experimental import pallas as pl
from jax.experimental.pallas import tpu as pltpu

F32 = jnp.float32
BF16 = jnp.bfloat16

D_MODEL = 1024
HEAD_DIM = 64
N_HEADS = 8
WIDTH = N_HEADS * HEAD_DIM
DECAY_LORA = 64
AAA_LORA = 64
GATE_LORA = 160
RW_COLS = 3 * WIDTH + DECAY_LORA + AAA_LORA + GATE_LORA
D_FF = 2816
RMS_EPS = 1e-6
GN_EPS = 64e-5

LANES = 128
RW_R, RW_K, RW_V = 0, WIDTH, 2 * WIDTH
RW_WD, RW_AD, RW_GD = 3 * WIDTH, 3 * WIDTH + LANES, 3 * WIDTH + 2 * LANES
RW_PAD = 3 * WIDTH + 4 * LANES
WD_SLOT, AD_SLOT, GD_SLOT = LANES, LANES, 2 * LANES

VMEM_LIMIT_BYTES = 56 * 1024 * 1024

RWKV_CHUNK = 64
SB_BLOCK = 256
SAMPLE_PAGES_PER_STEP = 8


def _dot(a, b):
    return jnp.dot(a, b, preferred_element_type=F32)


def _dot_nt(a, b):
    return lax.dot_general(a, b, (((1,), (1,)), ((), ())), preferred_element_type=F32)


def _dot_tn(a, b):
    return lax.dot_general(a, b, (((0,), (0,)), ((), ())), preferred_element_type=F32)


def _split(x):
    hi = x.astype(BF16)
    lo = (x - hi.astype(F32)).astype(BF16)
    return hi, lo


def _softplus(z):
    return jnp.maximum(z, 0.0) + jnp.log(1.0 + jnp.exp(-jnp.abs(z)))


def _sigmoid(x):
    return 1.0 / (1.0 + jnp.exp(-x))


def _const_spec(shape):
    nd = len(shape)
    return pl.BlockSpec(shape, lambda *_: (0,) * nd, pipeline_mode=pl.Buffered(1))


def _pad_rw(x):
    def slot(lo, width, size):
        part = x[..., lo:lo + width]
        pad = [(0, 0)] * (x.ndim - 1) + [(0, size - width)]
        return jnp.pad(part, pad)
    o = 3 * WIDTH
    return jnp.concatenate([
        x[..., :o],
        slot(o, DECAY_LORA, WD_SLOT),
        slot(o + DECAY_LORA, AAA_LORA, AD_SLOT),
        slot(o + DECAY_LORA + AAA_LORA, GATE_LORA, GD_SLOT),
    ], axis=-1)


def _unpad_rw(x):
    return jnp.concatenate([
        x[..., :3 * WIDTH],
        x[..., RW_WD:RW_WD + DECAY_LORA],
        x[..., RW_AD:RW_AD + AAA_LORA],
        x[..., RW_GD:RW_GD + GATE_LORA],
    ], axis=-1)


def _inproj_kernel(x_ref, g_ref, wq_ref, wk_ref, wv_ref, wrw_ref, wg_ref, bg_ref,
                   q_ref, k_ref, v_ref, kb_ref, vb_ref, rw_ref, gate_ref):
    x = x_ref[...]
    ms = jnp.mean(x * x, axis=-1, keepdims=True)
    h = (x * lax.rsqrt(ms + RMS_EPS) * g_ref[...]).astype(BF16)
    q_ref[...] = _dot(h, wq_ref[...]) * (HEAD_DIM ** -0.5)
    k = _dot(h, wk_ref[...])
    k_ref[...] = k
    kb_ref[...] = k.astype(BF16)
    v = _dot(h, wv_ref[...])
    v_ref[...] = v
    vb_ref[...] = v.astype(BF16)
    rw_ref[...] = _dot(h, wrw_ref[...])
    gate_ref[...] = _sigmoid(_dot(h, wg_ref[...]) + bg_ref[...])


def _inproj(x, g, wq, wk, wv, wrw, wg, bg, *, tm):
    n = x.shape[0]
    row = lambda w: pl.BlockSpec((tm, w), lambda i: (i, 0))
    return pl.pallas_call(
        _inproj_kernel,
        grid=(n // tm,),
        in_specs=[row(D_MODEL), _const_spec((1, D_MODEL)),
                  _const_spec((D_MODEL, WIDTH)), _const_spec((D_MODEL, WIDTH)), _const_spec((D_MODEL, WIDTH)),
                  _const_spec((D_MODEL, RW_PAD)), _const_spec((D_MODEL, 2 * D_MODEL)),
                  _const_spec((1, 2 * D_MODEL))],
        out_specs=[row(WIDTH), row(WIDTH), row(WIDTH), row(WIDTH), row(WIDTH), row(RW_PAD), row(2 * D_MODEL)],
        out_shape=[jax.ShapeDtypeStruct((n, WIDTH), F32),
                   jax.ShapeDtypeStruct((n, WIDTH), F32),
                   jax.ShapeDtypeStruct((n, WIDTH), F32),
                   jax.ShapeDtypeStruct((n, WIDTH), BF16),
                   jax.ShapeDtypeStruct((n, WIDTH), BF16),
                   jax.ShapeDtypeStruct((n, RW_PAD), F32),
                   jax.ShapeDtypeStruct((n, 2 * D_MODEL), F32)],
        compiler_params=pltpu.CompilerParams(dimension_semantics=("parallel",),
                                             vmem_limit_bytes=VMEM_LIMIT_BYTES),
        name="inproj",
    )(x, g, wq, wk, wv, wrw, wg, bg)


def _sb_prompt_kernel(bias_ref, q_ref, k_ref, v_ref, o_ref, acc_ref, *, tq):
    pair = pl.program_id(1)
    qi = pl.program_id(2)
    lane = lax.broadcasted_iota(jnp.int32, (1, LANES), 1)
    q = q_ref[...]
    qh = (jnp.where(lane < HEAD_DIM, q, 0.0).astype(BF16), jnp.where(lane >= HEAD_DIM, q, 0.0).astype(BF16))
    bias = (bias_ref[2 * pair], bias_ref[2 * pair + 1])
    r = lax.broadcasted_iota(jnp.int32, (tq, tq), 0)
    c = lax.broadcasted_iota(jnp.int32, (tq, tq), 1)
    tri = (r >= c).astype(BF16)
    visible = c < r
    acc_ref[...] = jnp.zeros_like(acc_ref)

    def block(kb, carry, masked):
        start = pl.multiple_of(kb * tq, tq)
        k = k_ref[pl.ds(start, tq), :]
        v = v_ref[pl.ds(start, tq), :]
        out = []
        for h in range(2):
            z = _dot_nt(qh[h], k) + bias[h]
            sp = _softplus(z)
            if masked:
                sp = jnp.where(visible, sp, 0.0)
            hi, lo = _split(sp)
            tail = _dot(hi, tri) + _dot(lo, tri) + carry[h]
            att = jnp.exp(z - tail)
            if masked:
                att = jnp.where(visible, att, 0.0)
            acc_ref[h] += _dot(att.astype(BF16), v)
            out.append(tail[:, 0:1])
        return tuple(out)

    zero = jnp.zeros((tq, 1), F32)
    carry = block(qi, (zero, zero), True)
    lax.fori_loop(0, qi, lambda it, cr: block(qi - 1 - it, cr, False), carry)
    o_ref[...] = jnp.where(lane < HEAD_DIM, acc_ref[0], acc_ref[1]).astype(o_ref.dtype)


def _sb_prompt(q, kb, vb, bias, *, tq):
    b, t, _ = q.shape
    kv_spec = pl.BlockSpec((None, t, LANES), lambda bi, p, i: (bi, 0, p))
    return pl.pallas_call(
        functools.partial(_sb_prompt_kernel, tq=tq),
        grid=(b, WIDTH // LANES, t // tq),
        in_specs=[pl.BlockSpec(memory_space=pltpu.SMEM),
                  pl.BlockSpec((None, tq, LANES), lambda bi, p, i: (bi, i, p)),
                  kv_spec, kv_spec],
        out_specs=pl.BlockSpec((None, tq, LANES), lambda bi, p, i: (bi, i, p)),
        out_shape=jax.ShapeDtypeStruct((b, t, WIDTH), BF16),
        scratch_shapes=[pltpu.VMEM((2, tq, LANES), F32)],
        compiler_params=pltpu.CompilerParams(dimension_semantics=("parallel", "parallel", "arbitrary"),
                                             vmem_limit_bytes=VMEM_LIMIT_BYTES),
        name="sb_prompt",
    )(bias, q, kb, vb)


def _sb_sample_kernel(pt_ref, q_ref, kn_ref, vn_ref, bias_ref, *rest, n_pages_step, t_new, page):
    kt_refs = rest[:n_pages_step]
    vt_refs = rest[n_pages_step:2 * n_pages_step]
    o_ref, qbd_ref, c_ref, acc_ref = rest[2 * n_pages_step:]
    del pt_ref
    j = pl.program_id(1)
    rows = N_HEADS * t_new
    r = lax.broadcasted_iota(jnp.int32, (page, page), 0)
    c = lax.broadcasted_iota(jnp.int32, (page, page), 1)
    tri = (r >= c).astype(BF16)
    bias = bias_ref[...]

    def block(z, av, visible):
        z = z + bias
        sp = _softplus(z)
        if visible is not None:
            sp = jnp.where(visible, sp, 0.0)
        hi, lo = _split(sp)
        tail = _dot(hi, tri) + _dot(lo, tri) + c_ref[...]
        att = jnp.exp(z - tail)
        if visible is not None:
            att = jnp.where(visible, att, 0.0)
        acc_ref[...] += av(att.astype(BF16))
        c_ref[...] = jnp.broadcast_to(tail[:, 0:1], c_ref.shape)

    @pl.when(j == 0)
    def _():
        q = q_ref[...]
        qt = jnp.concatenate([q] * N_HEADS, axis=0)
        rr = lax.broadcasted_iota(jnp.int32, (rows, WIDTH), 0)
        cc = lax.broadcasted_iota(jnp.int32, (rows, WIDTH), 1)
        qbd_ref[...] = jnp.where(rr // t_new == cc // HEAD_DIM, qt, 0.0).astype(BF16)
        c_ref[...] = jnp.zeros_like(c_ref)
        acc_ref[...] = jnp.zeros_like(acc_ref)
        pad = jnp.zeros((page - t_new, WIDTH), F32)
        kn = jnp.concatenate([kn_ref[...], pad], axis=0).astype(BF16)
        vn = jnp.concatenate([vn_ref[...], pad], axis=0).astype(BF16)
        qrow = lax.broadcasted_iota(jnp.int32, (rows, page), 0) % t_new
        key = lax.broadcasted_iota(jnp.int32, (rows, page), 1)
        block(_dot_nt(qbd_ref[...], kn), lambda att: _dot(att, vn), key < qrow)

    for g in range(n_pages_step):
        kt = kt_refs[g][...].astype(BF16)
        vt = vt_refs[g][...].astype(BF16)
        block(_dot(qbd_ref[...], kt), lambda att, vt=vt: _dot_nt(att, vt), None)

    @pl.when(j == pl.num_programs(1) - 1)
    def _():
        lane_head = lax.broadcasted_iota(jnp.int32, (t_new, WIDTH), 1) // HEAD_DIM
        out = jnp.zeros((t_new, WIDTH), F32)
        for h in range(N_HEADS):
            out = out + jnp.where(lane_head == h, acc_ref[h * t_new:(h + 1) * t_new, :], 0.0)
        o_ref[...] = out.astype(o_ref.dtype)


def _sb_sample(q, k_new, v_new, cache_kt, cache_vt, page_table, bias_rows):
    s, t_new, _ = q.shape
    n_pages = page_table.shape[1]
    page = cache_kt.shape[2]
    g_pages = SAMPLE_PAGES_PER_STEP
    assert n_pages % g_pages == 0
    rows = N_HEADS * t_new

    def page_spec(g):
        return pl.BlockSpec((None, WIDTH, page),
                            lambda n, j, pt: (pt[n, n_pages - 1 - (j * g_pages + g)], 0, 0))

    seq_spec = pl.BlockSpec((None, t_new, WIDTH), lambda n, j, pt: (n, 0, 0))
    grid_spec = pltpu.PrefetchScalarGridSpec(
        num_scalar_prefetch=1,
        grid=(s, n_pages // g_pages),
        in_specs=[seq_spec, seq_spec, seq_spec,
                  pl.BlockSpec((rows, 1), lambda n, j, pt: (0, 0))]
                 + [page_spec(g) for g in range(g_pages)] * 2,
        out_specs=seq_spec,
        scratch_shapes=[pltpu.VMEM((rows, WIDTH), BF16),
                        pltpu.VMEM((rows, page), F32),
                        pltpu.VMEM((rows, WIDTH), F32)],
    )
    return pl.pallas_call(
        functools.partial(_sb_sample_kernel, n_pages_step=g_pages, t_new=t_new, page=page),
        grid_spec=grid_spec,
        out_shape=jax.ShapeDtypeStruct((s, t_new, WIDTH), BF16),
        compiler_params=pltpu.CompilerParams(dimension_semantics=("parallel", "arbitrary"),
                                             vmem_limit_bytes=VMEM_LIMIT_BYTES),
        name="sb_sample",
    )(page_table, q, k_new, v_new, bias_rows, *([cache_kt] * g_pages), *([cache_vt] * g_pages))


def _rwkv_kernel(rw_ref, prev0_ref, s0_ref, mu_ref, w0_ref, a0_ref, kk_ref, ka_ref, rk_ref, lnw_ref, lnb_ref,
                 wd_ref, wa_ref, wg_ref, e_ref,
                 o_ref, sout_ref,
                 s_scr, prev_scr, u_scr, y_scr, *, tb, chunk):
    j = pl.program_id(1)

    @pl.when(j == 0)
    def _():
        s_scr[...] = s0_ref[...]
        prev_scr[...] = prev0_ref[...]

    rw = rw_ref[...]
    if tb < chunk:
        rw = jnp.concatenate([rw, jnp.zeros((chunk - tb, RW_PAD), F32)], axis=0)
    row = lax.broadcasted_iota(jnp.int32, (chunk, 1), 0)
    prev = jnp.where(row == 0, prev_scr[...], pltpu.roll(rw, 1, axis=0))
    xs = rw + (prev - rw) * mu_ref[...]
    prev_scr[...] = rw[tb - 1:tb, :]

    r = xs[:, RW_R:RW_R + WIDTH]
    k = xs[:, RW_K:RW_K + WIDTH]
    v = xs[:, RW_V:RW_V + WIDTH]
    wd = xs[:, RW_WD:RW_WD + WD_SLOT]
    ad = xs[:, RW_AD:RW_AD + AD_SLOT]
    gd = xs[:, RW_GD:RW_GD + GD_SLOT]

    e_mat = e_ref[...]

    def head_sum(x):
        hi, lo = _split(x)
        return _dot(hi, e_mat) + _dot(lo, e_mat)

    x_w = w0_ref[...] + _dot(jnp.tanh(wd).astype(BF16), wd_ref[...])
    w_log = -_softplus(-x_w) - 0.5
    lw = -jnp.exp(w_log)
    a_sig = _sigmoid(a0_ref[...] + _dot(ad.astype(BF16), wa_ref[...]))
    g = _dot(_sigmoid(gd).astype(BF16), wg_ref[...])
    kkr = k * kk_ref[...]
    kk = kkr / jnp.maximum(jnp.sqrt(head_sum(kkr * kkr)), 1e-12)
    kmod = k * (1.0 + (a_sig - 1.0) * ka_ref[...])
    if tb < chunk:
        valid = row < tb
        lw = jnp.where(valid, lw, 0.0)
        kk = jnp.where(valid, kk, 0.0)
        kmod = jnp.where(valid, kmod, 0.0)

    lc = lw
    shift = 1
    while shift < chunk:
        lc = lc + jnp.where(row >= shift, pltpu.roll(lc, shift, axis=0), 0.0)
        shift *= 2
    lprev = lc - lw
    lref = lc[chunk // 2 - 1:chunk // 2, :]
    lend = lc[chunk - 1:chunk, :]
    a_hat = -kk * jnp.exp(lprev - lref)
    a_start = -kk * jnp.exp(lprev)
    r_hat = r * jnp.exp(lc - lref)
    r_start = r * jnp.exp(lc)
    e_b = jnp.exp(lref - lc)
    b_vec = kk * a_sig
    b_hat = b_vec * e_b
    k_hat = kmod * e_b
    e_e = jnp.exp(lend - lc)
    b_til = b_vec * e_e
    k_til = kmod * e_e
    dec_end = jnp.exp(lend)

    ti = lax.broadcasted_iota(jnp.int32, (chunk, chunk), 0)
    tj = lax.broadcasted_iota(jnp.int32, (chunk, chunk), 1)

    bf = lambda x: x.astype(BF16)
    y_parts = []
    for h in range(N_HEADS):
        sl = slice(h * HEAD_DIM, (h + 1) * HEAD_DIM)
        ah, rh, bh, kh = bf(a_hat[:, sl]), bf(r_hat[:, sl]), bf(b_hat[:, sl]), bf(k_hat[:, sl])
        vh = v[:, sl]
        vh_b = bf(vh)
        s0 = s_scr[h]
        s0_b = bf(s0)
        m_ak = jnp.where(tj < ti, _dot_nt(ah, kh), 0.0)
        mt_ab = jnp.where(ti < tj, _dot_nt(bh, ah), 0.0)
        u_scr[h] = _dot_nt(bf(a_start[:, sl]), s0_b) + _dot(bf(m_ak), vh_b)
        for t in range(1, chunk):
            n = -(-t // 8) * 8
            col = mt_ab[:n, t:t + 1]
            u_scr[h, t:t + 1, :] += jnp.sum(col * u_scr[h, :n, :], axis=0, keepdims=True)
        u = u_scr[h]
        u_b = bf(u)
        m_rb = jnp.where(tj <= ti, _dot_nt(rh, bh), 0.0)
        m_rk = jnp.where(tj <= ti, _dot_nt(rh, kh), 0.0)
        y_parts.append(_dot_nt(bf(r_start[:, sl]), s0_b) + _dot(bf(m_rb), u_b) + _dot(bf(m_rk), vh_b))
        s_scr[h] = s0 * dec_end[:, sl] + _dot_tn(u_b, bf(b_til[:, sl])) + _dot_tn(vh_b, bf(k_til[:, sl]))
    y = jnp.concatenate(y_parts, axis=-1)

    inv_n = 1.0 / HEAD_DIM
    mean = head_sum(y) * inv_n
    d = y - mean
    var = head_sum(d * d) * inv_n
    yn = d * lax.rsqrt(var + GN_EPS) * lnw_ref[...] + lnb_ref[...]
    bonus = head_sum(r * kmod * rk_ref[...]) * v
    out = (yn + bonus) * g
    o_ref[...] = out[:tb].astype(o_ref.dtype)

    @pl.when(j == pl.num_programs(1) - 1)
    def _():
        sout_ref[...] = s_scr[...]


def _rwkv(rw, prev0, s0, mu, w0, a0, k_k, k_a, r_k, lnx_w, lnx_b, wd_up, wa_up, wg_up, e_mat, *, tb):
    b, t, _ = rw.shape
    chunk = RWKV_CHUNK
    assert t % tb == 0 and tb <= chunk
    vec = _const_spec((1, WIDTH))
    return pl.pallas_call(
        functools.partial(_rwkv_kernel, tb=tb, chunk=chunk),
        grid=(b, t // tb),
        in_specs=[pl.BlockSpec((None, tb, RW_PAD), lambda bi, j: (bi, j, 0)),
                  pl.BlockSpec((None, 1, RW_PAD), lambda bi, j: (bi, 0, 0)),
                  pl.BlockSpec((None, N_HEADS, HEAD_DIM, HEAD_DIM), lambda bi, j: (bi, 0, 0, 0)),
                  _const_spec((1, RW_PAD)),
                  vec, vec, vec, vec, vec, vec, vec,
                  _const_spec((WD_SLOT, WIDTH)), _const_spec((AD_SLOT, WIDTH)), _const_spec((GD_SLOT, WIDTH)),
                  _const_spec((WIDTH, WIDTH))],
        out_specs=[pl.BlockSpec((None, tb, WIDTH), lambda bi, j: (bi, j, 0)),
                   pl.BlockSpec((None, N_HEADS, HEAD_DIM, HEAD_DIM), lambda bi, j: (bi, 0, 0, 0))],
        out_shape=[jax.ShapeDtypeStruct((b, t, WIDTH), BF16),
                   jax.ShapeDtypeStruct((b, N_HEADS, HEAD_DIM, HEAD_DIM), F32)],
        scratch_shapes=[pltpu.VMEM((N_HEADS, HEAD_DIM, HEAD_DIM), F32),
                        pltpu.VMEM((1, RW_PAD), F32),
                        pltpu.VMEM((N_HEADS, chunk, HEAD_DIM), F32),
                        pltpu.VMEM((chunk, WIDTH), F32)],
        compiler_params=pltpu.CompilerParams(dimension_semantics=("parallel", "arbitrary"),
                                             vmem_limit_bytes=VMEM_LIMIT_BYTES),
        name="rwkv",
    )(rw, prev0, s0, mu, w0, a0, k_k, k_a, r_k, lnx_w, lnx_b, wd_up, wa_up, wg_up, e_mat)


def _merge_ffn_kernel(x_ref, osb_ref, orw_ref, gate_ref, wsb_ref, wrw_ref, wo_ref, gffn_ref, win_ref, wout_ref,
                      gfin_ref, y_ref, *, ff_chunk):
    gates = gate_ref[...]
    merged = (gates[:, :D_MODEL] * _dot(osb_ref[...], wsb_ref[...])
              + gates[:, D_MODEL:] * _dot(orw_ref[...], wrw_ref[...]))
    x = x_ref[...] + _dot(merged.astype(BF16), wo_ref[...])
    ms = jnp.mean(x * x, axis=-1, keepdims=True)
    h = (x * lax.rsqrt(ms + RMS_EPS) * gffn_ref[...]).astype(BF16)
    for c in range(D_FF // ff_chunk):
        lo = c * ff_chunk
        gate = _dot(h, win_ref[:, lo:lo + ff_chunk])
        up = _dot(h, win_ref[:, D_FF + lo:D_FF + lo + ff_chunk])
        act = (gate * _sigmoid(gate) * up).astype(BF16)
        x = x + _dot(act, wout_ref[lo:lo + ff_chunk, :])
    ms = jnp.mean(x * x, axis=-1, keepdims=True)
    y_ref[...] = x * lax.rsqrt(ms + RMS_EPS) * gfin_ref[...]


def _merge_ffn(x, o_sb, o_rw, gates, w_up_sb, w_up_rw, w_o, g_ffn, w_ffn_in, w_ffn_out, g_final, *, tm):
    n = x.shape[0]
    row = lambda w: pl.BlockSpec((tm, w), lambda i: (i, 0))
    return pl.pallas_call(
        functools.partial(_merge_ffn_kernel, ff_chunk=D_FF // 2),
        grid=(n // tm,),
        in_specs=[row(D_MODEL), row(WIDTH), row(WIDTH), row(2 * D_MODEL),
                  _const_spec((WIDTH, D_MODEL)), _const_spec((WIDTH, D_MODEL)), _const_spec((D_MODEL, D_MODEL)),
                  _const_spec((1, D_MODEL)), _const_spec((D_MODEL, 2 * D_FF)), _const_spec((D_FF, D_MODEL)),
                  _const_spec((1, D_MODEL))],
        out_specs=row(D_MODEL),
        out_shape=jax.ShapeDtypeStruct((n, D_MODEL), F32),
        compiler_params=pltpu.CompilerParams(dimension_semantics=("parallel",),
                                             vmem_limit_bytes=VMEM_LIMIT_BYTES),
        name="merge_ffn",
    )(x, o_sb, o_rw, gates, w_up_sb, w_up_rw, w_o, g_ffn, w_ffn_in, w_ffn_out, g_final)


def kernel(x_prompt, x_sample, cache_k, cache_v, page_table, state_wkv, state_shift, norm_mix_g, w_in, sb_bias, b_gate, mu_shift, w0, w_decay_up, a0, w_a_up, w_g_up, k_k, k_a, r_k, lnx_w, lnx_b, w_up_sb, w_up_rw, w_o, norm_ffn_g, w_ffn_in, w_ffn_out, norm_final_g):
    depth = w_in.shape[0]
    assert depth == 1
    bp, tp, _ = x_prompt.shape
    bs, ts, _ = x_sample.shape
    l = 0

    w = w_in[l]
    att_cols = 3 * WIDTH
    wq = w[:, 0:WIDTH].astype(BF16)
    wk = w[:, WIDTH:2 * WIDTH].astype(BF16)
    wv = w[:, 2 * WIDTH:att_cols].astype(BF16)
    wrw = _pad_rw(w[:, att_cols:att_cols + RW_COLS]).astype(BF16)
    wg = w[:, att_cols + RW_COLS:].astype(BF16)
    bg = b_gate[l][None, :]
    g_mix = norm_mix_g[l][None, :]
    mu = _pad_rw(mu_shift[l])[None, :]
    pad_rows = lambda m, n: jnp.pad(m, ((0, n - m.shape[0]), (0, 0))).astype(BF16)
    wd_up = pad_rows(w_decay_up[l], WD_SLOT)
    wa_up = pad_rows(w_a_up[l], AD_SLOT)
    wg_up = pad_rows(w_g_up[l], GD_SLOT)
    vec = lambda p: p.reshape(1, WIDTH)
    rw_vecs = (vec(w0[l]), vec(a0[l]), vec(k_k[l]), vec(k_a[l]), vec(r_k[l]), vec(lnx_w[l]), vec(lnx_b[l]))
    ch = jnp.arange(WIDTH, dtype=jnp.int32) // HEAD_DIM
    e_mat = (ch[:, None] == ch[None, :]).astype(BF16)
    ffn_w = (w_up_sb[l].astype(BF16), w_up_rw[l].astype(BF16), w_o[l].astype(BF16), norm_ffn_g[l][None, :],
             w_ffn_in[l].astype(BF16), w_ffn_out[l].astype(BF16), norm_final_g[None, :])
    bias = sb_bias[l]

    def group(x, tm):
        n = x.shape[0] * x.shape[1]
        return _inproj(x.reshape(n, D_MODEL), g_mix, wq, wk, wv, wrw, wg, bg, tm=tm)

    q, k, v, kb, vb, rw, gates = group(x_prompt, 512)
    sh = (bp, tp, WIDTH)
    o_sb = _sb_prompt(q.reshape(sh), kb.reshape(sh), vb.reshape(sh), bias, tq=SB_BLOCK)
    rw3 = rw.reshape(bp, tp, RW_PAD)
    o_rw, wkv_p = _rwkv(rw3, jnp.zeros((bp, 1, RW_PAD), F32), jnp.zeros((bp, N_HEADS, HEAD_DIM, HEAD_DIM), F32),
                        mu, *rw_vecs, wd_up, wa_up, wg_up, e_mat, tb=RWKV_CHUNK)
    y_p = _merge_ffn(x_prompt.reshape(bp * tp, D_MODEL), o_sb.reshape(bp * tp, WIDTH), o_rw.reshape(bp * tp, WIDTH),
                     gates, *ffn_w, tm=512)
    k_prompt = k.reshape(1, bp, tp, N_HEADS, HEAD_DIM)
    v_prompt = v.reshape(1, bp, tp, N_HEADS, HEAD_DIM)
    shift_p = _unpad_rw(rw3[:, -1, :])[None]

    q, k, v, kb, vb, rw, gates = group(x_sample, 512)
    sh = (bs, ts, WIDTH)
    n_phys, page = cache_k.shape[1], cache_k.shape[2]
    bias_rows = jnp.repeat(bias, ts)[:, None]
    pages_t = lambda c: jnp.transpose(c, (0, 1, 3, 4, 2)).reshape(depth * n_phys, WIDTH, page)
    o_sb = _sb_sample(q.reshape(sh), k.reshape(sh), v.reshape(sh), pages_t(cache_k), pages_t(cache_v),
                      page_table + l * n_phys, bias_rows)
    rw3 = rw.reshape(bs, ts, RW_PAD)
    o_rw, wkv_s = _rwkv(rw3, _pad_rw(state_shift[l])[:, None, :], state_wkv[l],
                        mu, *rw_vecs, wd_up, wa_up, wg_up, e_mat, tb=ts)
    y_s = _merge_ffn(x_sample.reshape(bs * ts, D_MODEL), o_sb.reshape(bs * ts, WIDTH), o_rw.reshape(bs * ts, WIDTH),
                     gates, *ffn_w, tm=512)
    k_sample = k.reshape(1, bs, ts, N_HEADS, HEAD_DIM)
    v_sample = v.reshape(1, bs, ts, N_HEADS, HEAD_DIM)
    shift_s = _unpad_rw(rw3[:, -1, :])[None]

    return (y_p.reshape(bp, tp, D_MODEL), y_s.reshape(bs, ts, D_MODEL),
            k_prompt, v_prompt, wkv_p[None], shift_p,
            k_sample, v_sample, wkv_s[None], shift_s)
```

```python
import functools

import jax
import jax.numpy as jnp
from jax import lax
from jax.experimental import pallas as pl
from jax.experimental.pallas import tpu as pltpu

F32 = jnp.float32
BF16 = jnp.bfloat16

D_MODEL = 1024
HEAD_DIM = 64
N_HEADS = 8
WIDTH = N_HEADS * HEAD_DIM
DECAY_LORA = 64
AAA_LORA = 64
GATE_LORA = 160
RW_COLS = 3 * WIDTH + DECAY_LORA + AAA_LORA + GATE_LORA
D_FF = 2816
RMS_EPS = 1e-6
GN_EPS = 64e-5
LOG2_E = 1.4426950408889634

LANES = 128
RW_R, RW_K, RW_V = 0, WIDTH, 2 * WIDTH
RW_WD, RW_AD, RW_GD = 3 * WIDTH, 3 * WIDTH + LANES, 3 * WIDTH + 2 * LANES
RW_PAD = 3 * WIDTH + 4 * LANES
WD_SLOT, AD_SLOT, GD_SLOT = LANES, LANES, 2 * LANES

VMEM_LIMIT_BYTES = 56 * 1024 * 1024

RWKV_CHUNK = 64
RWKV_SUBBLOCK = 16
RWKV_PROMPT_SEQS = 2
RWKV_SAMPLE_SEQS = 8
SB_BLOCK = 256
SAMPLE_PAGES_PER_STEP = 8


def _dot(a, b):
    return jnp.dot(a, b, preferred_element_type=F32)


def _dot_nt(a, b):
    return lax.dot_general(a, b, (((1,), (1,)), ((), ())), preferred_element_type=F32)


def _dot_tn(a, b):
    return lax.dot_general(a, b, (((0,), (0,)), ((), ())), preferred_element_type=F32)


def _split(x):
    hi = x.astype(BF16)
    lo = (x - hi.astype(F32)).astype(BF16)
    return hi, lo


def _dot3(a, b):
    a_hi, a_lo = _split(a)
    b_hi, b_lo = _split(b)
    return _dot(a_hi, b_hi) + _dot(a_lo, b_hi) + _dot(a_hi, b_lo)


def _softplus(z):
    return jnp.maximum(z, 0.0) + jnp.log(1.0 + jnp.exp2(jnp.abs(z) * (-LOG2_E)))


def _sigmoid(x):
    return 1.0 / (1.0 + jnp.exp(-x))


def _const_spec(shape):
    nd = len(shape)
    return pl.BlockSpec(shape, lambda *_: (0,) * nd, pipeline_mode=pl.Buffered(1))


def _pad_rw(x):
    def slot(lo, width, size):
        part = x[..., lo:lo + width]
        pad = [(0, 0)] * (x.ndim - 1) + [(0, size - width)]
        return jnp.pad(part, pad)
    o = 3 * WIDTH
    return jnp.concatenate([
        x[..., :o],
        slot(o, DECAY_LORA, WD_SLOT),
        slot(o + DECAY_LORA, AAA_LORA, AD_SLOT),
        slot(o + DECAY_LORA + AAA_LORA, GATE_LORA, GD_SLOT),
    ], axis=-1)


def _unpad_rw(x):
    return jnp.concatenate([
        x[..., :3 * WIDTH],
        x[..., RW_WD:RW_WD + DECAY_LORA],
        x[..., RW_AD:RW_AD + AAA_LORA],
        x[..., RW_GD:RW_GD + GATE_LORA],
    ], axis=-1)


def _inproj_kernel(x_ref, g_ref, wq_ref, wk_ref, wv_ref, wrw_ref, wg_ref, bg_ref,
                   q_ref, k_ref, v_ref, kb_ref, vb_ref, rw_ref, gate_ref):
    x = x_ref[...]
    ms = jnp.mean(x * x, axis=-1, keepdims=True)
    h = (x * lax.rsqrt(ms + RMS_EPS) * g_ref[...]).astype(BF16)
    q_ref[...] = _dot(h, wq_ref[...]) * (HEAD_DIM ** -0.5)
    k = _dot(h, wk_ref[...])
    k_ref[...] = k
    kb_ref[...] = k.astype(BF16)
    v = _dot(h, wv_ref[...])
    v_ref[...] = v
    vb_ref[...] = v.astype(BF16)
    rw_ref[...] = _dot(h, wrw_ref[...])
    gate_ref[...] = _sigmoid(_dot(h, wg_ref[...]) + bg_ref[...])


def _inproj(x, g, wq, wk, wv, wrw, wg, bg, *, tm):
    n = x.shape[0]
    row = lambda w: pl.BlockSpec((tm, w), lambda i: (i, 0))
    return pl.pallas_call(
        _inproj_kernel,
        grid=(n // tm,),
        in_specs=[row(D_MODEL), _const_spec((1, D_MODEL)),
                  _const_spec((D_MODEL, WIDTH)), _const_spec((D_MODEL, WIDTH)), _const_spec((D_MODEL, WIDTH)),
                  _const_spec((D_MODEL, RW_PAD)), _const_spec((D_MODEL, 2 * D_MODEL)),
                  _const_spec((1, 2 * D_MODEL))],
        out_specs=[row(WIDTH), row(WIDTH), row(WIDTH), row(WIDTH), row(WIDTH), row(RW_PAD), row(2 * D_MODEL)],
        out_shape=[jax.ShapeDtypeStruct((n, WIDTH), F32),
                   jax.ShapeDtypeStruct((n, WIDTH), F32),
                   jax.ShapeDtypeStruct((n, WIDTH), F32),
                   jax.ShapeDtypeStruct((n, WIDTH), BF16),
                   jax.ShapeDtypeStruct((n, WIDTH), BF16),
                   jax.ShapeDtypeStruct((n, RW_PAD), F32),
                   jax.ShapeDtypeStruct((n, 2 * D_MODEL), F32)],
        compiler_params=pltpu.CompilerParams(dimension_semantics=("parallel",),
                                             vmem_limit_bytes=VMEM_LIMIT_BYTES),
        name="inproj",
    )(x, g, wq, wk, wv, wrw, wg, bg)


def _sb_prompt_kernel(bias_ref, q_ref, k_ref, v_ref, o_ref, acc_ref, d_scr, sp_scr, att_scr, *, tq):
    pair = pl.program_id(1)
    qi = pl.program_id(2)
    lane = lax.broadcasted_iota(jnp.int32, (1, LANES), 1)
    q = q_ref[...]
    qh = (jnp.where(lane < HEAD_DIM, q, 0.0).astype(BF16), jnp.where(lane >= HEAD_DIM, q, 0.0).astype(BF16))
    bias = (bias_ref[2 * pair], bias_ref[2 * pair + 1])
    r = lax.broadcasted_iota(jnp.int32, (tq, tq), 0)
    c = lax.broadcasted_iota(jnp.int32, (tq, tq), 1)
    tri = (r > c).astype(BF16)
    acc_ref[...] = jnp.zeros_like(acc_ref)
    att_scr[...] = jnp.zeros_like(att_scr)

    def key_block(p):
        start = pl.multiple_of(jnp.clip(qi - p, 0, qi) * tq, tq)
        return pl.ds(start, tq)

    def scores(p, visible):
        k = k_ref[key_block(p), :]
        for h in range(2):
            z = _dot_nt(qh[h], k) + bias[h]
            if visible is not None:
                z = jnp.where(visible, z, -1e30)
            sp = _softplus(z)
            d_scr[h] = z - sp
            sp_scr[h] = sp.astype(BF16)

    def weights(carry):
        out = []
        for h in range(2):
            sp = sp_scr[h]
            tail = _dot(sp, tri) + carry[h]
            att_scr[h] = jnp.exp(d_scr[h] - tail).astype(BF16)
            out.append(tail[:, 0:1] + sp[:, 0:1].astype(F32))
        return tuple(out)

    def values(p):
        v = v_ref[key_block(p), :]
        for h in range(2):
            acc_ref[h] += _dot(att_scr[h], v)

    def trip(it, carry):
        values(it - 2)
        carry = weights(carry)
        scores(it, None)
        return carry

    scores(0, c < r)
    zero = jnp.zeros((tq, 1), F32)
    lax.fori_loop(1, qi + 3, trip, (zero, zero))
    o_ref[...] = jnp.where(lane < HEAD_DIM, acc_ref[0], acc_ref[1]).astype(o_ref.dtype)


def _sb_prompt(q, kb, vb, bias, *, tq):
    b, t, _ = q.shape
    kv_spec = pl.BlockSpec((None, t, LANES), lambda bi, p, i: (bi, 0, p))
    return pl.pallas_call(
        functools.partial(_sb_prompt_kernel, tq=tq),
        grid=(b, WIDTH // LANES, t // tq),
        in_specs=[pl.BlockSpec(memory_space=pltpu.SMEM),
                  pl.BlockSpec((None, tq, LANES), lambda bi, p, i: (bi, i, p)),
                  kv_spec, kv_spec],
        out_specs=pl.BlockSpec((None, tq, LANES), lambda bi, p, i: (bi, i, p)),
        out_shape=jax.ShapeDtypeStruct((b, t, WIDTH), BF16),
        scratch_shapes=[pltpu.VMEM((2, tq, LANES), F32), pltpu.VMEM((2, tq, tq), F32),
                        pltpu.VMEM((2, tq, tq), BF16), pltpu.VMEM((2, tq, tq), BF16)],
        compiler_params=pltpu.CompilerParams(dimension_semantics=("parallel", "parallel", "arbitrary"),
                                             vmem_limit_bytes=VMEM_LIMIT_BYTES),
        name="sb_prompt",
    )(bias, q, kb, vb)


def _sb_sample_kernel(pt_ref, q_ref, kn_ref, vn_ref, bias_ref, *rest, n_pages_step, t_new, page):
    kt_refs = rest[:n_pages_step]
    vt_refs = rest[n_pages_step:2 * n_pages_step]
    o_ref, qbd_ref, c_ref, acc_ref = rest[2 * n_pages_step:]
    del pt_ref
    j = pl.program_id(1)
    rows = N_HEADS * t_new
    r = lax.broadcasted_iota(jnp.int32, (page, page), 0)
    c = lax.broadcasted_iota(jnp.int32, (page, page), 1)
    tri = (r > c).astype(BF16)
    bias = bias_ref[...]

    def blocks(zs, avs):
        zs = [z + bias for z in zs]
        sps = [_softplus(z) for z in zs]
        sps_b = [sp.astype(BF16) for sp in sps]
        incl = [sp + _dot(sp_b, tri) for sp, sp_b in zip(sps, sps_b)]
        totals = [jnp.broadcast_to(t[:, 0:1], t.shape) for t in incl]
        run = c_ref[...]
        acc = acc_ref[...]
        for z, t, tot, av in zip(zs, incl, totals, avs):
            acc = acc + av(jnp.exp(z - (t + run)).astype(BF16))
            run = run + tot
        acc_ref[...] = acc
        c_ref[...] = run

    @pl.when(j == 0)
    def _():
        q = q_ref[...]
        qt = jnp.concatenate([q] * N_HEADS, axis=0)
        rr = lax.broadcasted_iota(jnp.int32, (rows, WIDTH), 0)
        cc = lax.broadcasted_iota(jnp.int32, (rows, WIDTH), 1)
        qbd_ref[...] = jnp.where(rr // t_new == cc // HEAD_DIM, qt, 0.0).astype(BF16)
        c_ref[...] = jnp.zeros_like(c_ref)
        acc_ref[...] = jnp.zeros_like(acc_ref)
        pad = jnp.zeros((page - t_new, WIDTH), F32)
        kn = jnp.concatenate([kn_ref[...], pad], axis=0).astype(BF16)
        vn = jnp.concatenate([vn_ref[...], pad], axis=0).astype(BF16)
        qrow = lax.broadcasted_iota(jnp.int32, (rows, page), 0) % t_new
        key = lax.broadcasted_iota(jnp.int32, (rows, page), 1)
        z = jnp.where(key < qrow, _dot_nt(qbd_ref[...], kn), -1e30)
        blocks([z], [lambda att: _dot(att, vn)])

    qbd = qbd_ref[...]
    blocks([_dot(qbd, kt_ref[...].astype(BF16)) for kt_ref in kt_refs],
           [lambda att, vt_ref=vt_ref: _dot_nt(att, vt_ref[...].astype(BF16)) for vt_ref in vt_refs])

    @pl.when(j == pl.num_programs(1) - 1)
    def _():
        lane_head = lax.broadcasted_iota(jnp.int32, (t_new, WIDTH), 1) // HEAD_DIM
        out = jnp.zeros((t_new, WIDTH), F32)
        for h in range(N_HEADS):
            out = out + jnp.where(lane_head == h, acc_ref[h * t_new:(h + 1) * t_new, :], 0.0)
        o_ref[...] = out.astype(o_ref.dtype)


def _sb_sample(q, k_new, v_new, cache_kt, cache_vt, page_table, bias_rows):
    s, t_new, _ = q.shape
    n_pages = page_table.shape[1]
    page = cache_kt.shape[2]
    g_pages = SAMPLE_PAGES_PER_STEP
    assert n_pages % g_pages == 0
    rows = N_HEADS * t_new

    def page_spec(g):
        return pl.BlockSpec((None, WIDTH, page),
                            lambda n, j, pt: (pt[n, n_pages - 1 - (j * g_pages + g)], 0, 0))

    seq_spec = pl.BlockSpec((None, t_new, WIDTH), lambda n, j, pt: (n, 0, 0))
    grid_spec = pltpu.PrefetchScalarGridSpec(
        num_scalar_prefetch=1,
        grid=(s, n_pages // g_pages),
        in_specs=[seq_spec, seq_spec, seq_spec,
                  pl.BlockSpec((rows, 1), lambda n, j, pt: (0, 0))]
                 + [page_spec(g) for g in range(g_pages)] * 2,
        out_specs=seq_spec,
        scratch_shapes=[pltpu.VMEM((rows, WIDTH), BF16),
                        pltpu.VMEM((rows, page), F32),
                        pltpu.VMEM((rows, WIDTH), F32)],
    )
    return pl.pallas_call(
        functools.partial(_sb_sample_kernel, n_pages_step=g_pages, t_new=t_new, page=page),
        grid_spec=grid_spec,
        out_shape=jax.ShapeDtypeStruct((s, t_new, WIDTH), BF16),
        compiler_params=pltpu.CompilerParams(dimension_semantics=("parallel", "arbitrary"),
                                             vmem_limit_bytes=VMEM_LIMIT_BYTES),
        name="sb_sample",
    )(page_table, q, k_new, v_new, bias_rows, *([cache_kt] * g_pages), *([cache_vt] * g_pages))


def _rwkv_kernel(rw_ref, prev0_ref, s0_ref, mu_ref, w0_ref, a0_ref, kk_ref, ka_ref, rk_ref, lnw_ref, lnb_ref,
                 wd_ref, wa_ref, wg_ref, e_ref,
                 o_ref, sout_ref,
                 sbd_scr, prev_scr, e_scr, *, nb, tb, chunk):
    assert chunk == HEAD_DIM
    j = pl.program_id(1)
    n_pairs = WIDTH // LANES
    low = lax.broadcasted_iota(jnp.int32, (1, LANES), 1) < HEAD_DIM

    @pl.when(j == 0)
    def _():
        prev_scr[...] = prev0_ref[...]
        zero = jnp.zeros((HEAD_DIM, HEAD_DIM), F32)
        for b in range(nb):
            for p in range(n_pairs):
                top = jnp.concatenate([s0_ref[b, 2 * p], zero], axis=1)
                bot = jnp.concatenate([zero, s0_ref[b, 2 * p + 1]], axis=1)
                sbd_scr[b, p] = jnp.concatenate([top, bot], axis=0)

    e_mat = e_ref[...]
    row = lax.broadcasted_iota(jnp.int32, (chunk, 1), 0)
    bf = lambda x: x.astype(BF16)

    def head_sum(x):
        hi, lo = _split(x)
        return _dot(hi, e_mat) + _dot(lo, e_mat)

    def prepare(b):
        rw = rw_ref[b]
        if tb < chunk:
            rw = jnp.concatenate([rw, jnp.zeros((chunk - tb, RW_PAD), F32)], axis=0)
        prev = jnp.where(row == 0, prev_scr[b], pltpu.roll(rw, 1, axis=0))
        xs = rw + (prev - rw) * mu_ref[...]
        prev_scr[b] = rw[tb - 1:tb, :]

        r = xs[:, RW_R:RW_R + WIDTH]
        k = xs[:, RW_K:RW_K + WIDTH]
        v = xs[:, RW_V:RW_V + WIDTH]
        wd = xs[:, RW_WD:RW_WD + WD_SLOT]
        ad = xs[:, RW_AD:RW_AD + AD_SLOT]
        gd = xs[:, RW_GD:RW_GD + GD_SLOT]

        x_w = w0_ref[...] + _dot(bf(jnp.tanh(wd)), wd_ref[...])
        w_log = -_softplus(-x_w) - 0.5
        lw = -jnp.exp(w_log)
        a_sig = _sigmoid(a0_ref[...] + _dot(bf(ad), wa_ref[...]))
        g = _dot(bf(_sigmoid(gd)), wg_ref[...])
        kkr = k * kk_ref[...]
        kk = kkr / jnp.maximum(jnp.sqrt(head_sum(kkr * kkr)), 1e-12)
        kmod = k * (1.0 + (a_sig - 1.0) * ka_ref[...])
        if tb < chunk:
            valid = row < tb
            lw = jnp.where(valid, lw, 0.0)
            kk = jnp.where(valid, kk, 0.0)
            kmod = jnp.where(valid, kmod, 0.0)

        lc = lw
        shift = 1
        while shift < chunk:
            lc = lc + jnp.where(row >= shift, pltpu.roll(lc, shift, axis=0), 0.0)
            shift *= 2
        lprev = lc - lw
        lref = lc[chunk // 2 - 1:chunk // 2, :]
        lend = lc[chunk - 1:chunk, :]
        e_b = jnp.exp(lref - lc)
        e_e = jnp.exp(lend - lc)
        b_vec = kk * a_sig
        return dict(
            r=r, v=v, kmod=kmod, g=g,
            a_hat=-kk * jnp.exp(lprev - lref), a_start=-kk * jnp.exp(lprev),
            r_hat=r * jnp.exp(lc - lref), r_start=r * jnp.exp(lc),
            b_hat=b_vec * e_b, k_hat=kmod * e_b, b_til=b_vec * e_e, k_til=kmod * e_e,
            dec_end=jnp.exp(lend))

    ss = lax.broadcasted_iota(jnp.int32, (chunk, LANES), 0)
    tt = lax.broadcasted_iota(jnp.int32, (chunk, LANES), 1) % HEAD_DIM
    same_head = (lax.broadcasted_iota(jnp.int32, (LANES, LANES), 0) // HEAD_DIM
                 == lax.broadcasted_iota(jnp.int32, (LANES, LANES), 1) // HEAD_DIM)

    def stack2(x):
        return jnp.concatenate([jnp.where(low, x, 0.0), jnp.where(low, 0.0, x)], axis=0)

    seqs = [prepare(b) for b in range(nb)]
    chains = [(b, p) for b in range(nb) for p in range(n_pairs)]
    part = lambda b, p, name: seqs[b][name][:, p * LANES:(p + 1) * LANES]

    sub = RWKV_SUBBLOCK
    n_sub = -(-min(chunk, tb) // sub)
    same_sub = ss // sub == tt // sub
    each = lambda f: [f(i, b, p) for i, (b, p) in enumerate(chains)]
    sbds = each(lambda i, b, p: sbd_scr[b, p])
    ahs = each(lambda i, b, p: bf(part(b, p, "a_hat")))
    bh2 = each(lambda i, b, p: bf(stack2(part(b, p, "b_hat"))))
    kh2 = each(lambda i, b, p: bf(stack2(part(b, p, "k_hat"))))
    v2 = each(lambda i, b, p: bf(stack2(part(b, p, "v"))))
    mts = each(lambda i, b, p: _dot_nt(bf(part(b, p, "b_hat")), bf(stack2(part(b, p, "a_hat")))))
    ms = each(lambda i, b, p: jnp.where(tt < ss, _dot_nt(ahs[i], bh2[i]), 0.0))
    m_aks = each(lambda i, b, p: jnp.where(tt < ss, _dot_nt(ahs[i], kh2[i]), 0.0))
    rhs0 = each(lambda i, b, p: _dot_nt(bf(part(b, p, "a_start")), bf(sbds[i])))
    rhss = each(lambda i, b, p: rhs0[i] + _dot(bf(m_aks[i]), v2[i]))
    mtds = [jnp.where(same_sub & (ss < tt), mt, 0.0) for mt in mts]
    for i in range(len(chains)):
        e_scr[i] = jnp.where(same_sub, ms[i], 0.0)

    for tau in range(1, min(sub, tb)):
        n = -(-tau // 8) * 8
        for i in range(len(chains)):
            for blk in range(n_sub):
                r0 = blk * sub
                t = r0 + tau
                coef = jnp.where(low, mtds[i][r0:r0 + n, t:t + 1], mtds[i][r0:r0 + n, HEAD_DIM + t:HEAD_DIM + t + 1])
                e_scr[i, t:t + 1, :] += jnp.sum(coef * e_scr[i, r0:r0 + n, :], axis=0, keepdims=True)

    es = each(lambda i, b, p: e_scr[i])
    size = sub
    while size < min(chunk, tb):
        coupled = ((ss // size) % 2 == 1) & (tt // size == ss // size - 1)
        ns = [jnp.where(coupled, m, 0.0) for m in ms]
        qs = each(lambda i, b, p: ns[i] + _dot3(es[i], stack2(ns[i])))
        es = each(lambda i, b, p: es[i] + qs[i] + _dot3(qs[i], stack2(es[i])))
        size *= 2
    us = each(lambda i, b, p: rhss[i] + _dot3(es[i], stack2(rhss[i])))

    rhs_b = each(lambda i, b, p: bf(part(b, p, "r_hat")))
    m_rbs = each(lambda i, b, p: jnp.where(tt <= ss, _dot_nt(rhs_b[i], bh2[i]), 0.0))
    m_rks = each(lambda i, b, p: jnp.where(tt <= ss, _dot_nt(rhs_b[i], kh2[i]), 0.0))
    y0 = each(lambda i, b, p: _dot_nt(bf(part(b, p, "r_start")), bf(sbds[i])))
    ys = each(lambda i, b, p: y0[i] + _dot(bf(jnp.concatenate([m_rbs[i], m_rks[i]], axis=1)),
                                          jnp.concatenate([bf(stack2(us[i])), v2[i]], axis=0)))
    upds = each(lambda i, b, p: _dot_tn(bf(jnp.concatenate([us[i], part(b, p, "v")], axis=0)),
                                        bf(jnp.concatenate([part(b, p, "b_til"), part(b, p, "k_til")], axis=0))))
    y_parts = [[] for _ in range(nb)]
    for i, (b, p) in enumerate(chains):
        y_parts[b].append(ys[i])
        sbd_scr[b, p] = jnp.where(same_head, sbds[i] * part(b, p, "dec_end") + upds[i], 0.0)

    inv_n = 1.0 / HEAD_DIM
    for b in range(nb):
        s = seqs[b]
        y = jnp.concatenate(y_parts[b], axis=-1)
        mean = head_sum(y) * inv_n
        d = y - mean
        var = head_sum(d * d) * inv_n
        yn = d * lax.rsqrt(var + GN_EPS) * lnw_ref[...] + lnb_ref[...]
        bonus = head_sum(s["r"] * s["kmod"] * rk_ref[...]) * s["v"]
        out = (yn + bonus) * s["g"]
        o_ref[b] = out[:tb].astype(o_ref.dtype)

    @pl.when(j == pl.num_programs(1) - 1)
    def _():
        for b in range(nb):
            for p in range(n_pairs):
                sbd = sbd_scr[b, p]
                sout_ref[b, 2 * p] = sbd[:HEAD_DIM, :HEAD_DIM]
                sout_ref[b, 2 * p + 1] = sbd[HEAD_DIM:, HEAD_DIM:]


def _rwkv(rw, prev0, s0, mu, w0, a0, k_k, k_a, r_k, lnx_w, lnx_b, wd_up, wa_up, wg_up, e_mat, *, tb, nb):
    b, t, _ = rw.shape
    chunk = RWKV_CHUNK
    assert t % tb == 0 and tb <= chunk and b % nb == 0
    vec = _const_spec((1, WIDTH))
    state_spec = pl.BlockSpec((nb, N_HEADS, HEAD_DIM, HEAD_DIM), lambda bi, j: (bi, 0, 0, 0))
    return pl.pallas_call(
        functools.partial(_rwkv_kernel, nb=nb, tb=tb, chunk=chunk),
        grid=(b // nb, t // tb),
        in_specs=[pl.BlockSpec((nb, tb, RW_PAD), lambda bi, j: (bi, j, 0)),
                  pl.BlockSpec((nb, 1, RW_PAD), lambda bi, j: (bi, 0, 0)),
                  state_spec,
                  _const_spec((1, RW_PAD)),
                  vec, vec, vec, vec, vec, vec, vec,
                  _const_spec((WD_SLOT, WIDTH)), _const_spec((AD_SLOT, WIDTH)), _const_spec((GD_SLOT, WIDTH)),
                  _const_spec((WIDTH, WIDTH))],
        out_specs=[pl.BlockSpec((nb, tb, WIDTH), lambda bi, j: (bi, j, 0)), state_spec],
        out_shape=[jax.ShapeDtypeStruct((b, t, WIDTH), BF16),
                   jax.ShapeDtypeStruct((b, N_HEADS, HEAD_DIM, HEAD_DIM), F32)],
        scratch_shapes=[pltpu.VMEM((nb, WIDTH // LANES, LANES, LANES), F32),
                        pltpu.VMEM((nb, 1, RW_PAD), F32),
                        pltpu.VMEM((nb * (WIDTH // LANES), chunk, LANES), F32)],
        compiler_params=pltpu.CompilerParams(dimension_semantics=("parallel", "arbitrary"),
                                             vmem_limit_bytes=VMEM_LIMIT_BYTES),
        name="rwkv",
    )(rw, prev0, s0, mu, w0, a0, k_k, k_a, r_k, lnx_w, lnx_b, wd_up, wa_up, wg_up, e_mat)


def _merge_ffn_kernel(x_ref, osb_ref, orw_ref, gate_ref, wsb_ref, wrw_ref, wo_ref, gffn_ref, win_ref, wout_ref,
                      gfin_ref, y_ref, *, ff_chunk):
    gates = gate_ref[...]
    merged = (gates[:, :D_MODEL] * _dot(osb_ref[...], wsb_ref[...])
              + gates[:, D_MODEL:] * _dot(orw_ref[...], wrw_ref[...]))
    x = x_ref[...] + _dot(merged.astype(BF16), wo_ref[...])
    ms = jnp.mean(x * x, axis=-1, keepdims=True)
    h = (x * lax.rsqrt(ms + RMS_EPS) * gffn_ref[...]).astype(BF16)
    for c in range(D_FF // ff_chunk):
        lo = c * ff_chunk
        gate = _dot(h, win_ref[:, lo:lo + ff_chunk])
        up = _dot(h, win_ref[:, D_FF + lo:D_FF + lo + ff_chunk])
        act = (gate * _sigmoid(gate) * up).astype(BF16)
        x = x + _dot(act, wout_ref[lo:lo + ff_chunk, :])
    ms = jnp.mean(x * x, axis=-1, keepdims=True)
    y_ref[...] = x * lax.rsqrt(ms + RMS_EPS) * gfin_ref[...]


def _merge_ffn(x, o_sb, o_rw, gates, w_up_sb, w_up_rw, w_o, g_ffn, w_ffn_in, w_ffn_out, g_final, *, tm):
    n = x.shape[0]
    row = lambda w: pl.BlockSpec((tm, w), lambda i: (i, 0))
    return pl.pallas_call(
        functools.partial(_merge_ffn_kernel, ff_chunk=D_FF // 2),
        grid=(n // tm,),
        in_specs=[row(D_MODEL), row(WIDTH), row(WIDTH), row(2 * D_MODEL),
                  _const_spec((WIDTH, D_MODEL)), _const_spec((WIDTH, D_MODEL)), _const_spec((D_MODEL, D_MODEL)),
                  _const_spec((1, D_MODEL)), _const_spec((D_MODEL, 2 * D_FF)), _const_spec((D_FF, D_MODEL)),
                  _const_spec((1, D_MODEL))],
        out_specs=row(D_MODEL),
        out_shape=jax.ShapeDtypeStruct((n, D_MODEL), F32),
        compiler_params=pltpu.CompilerParams(dimension_semantics=("parallel",),
                                             vmem_limit_bytes=VMEM_LIMIT_BYTES),
        name="merge_ffn",
    )(x, o_sb, o_rw, gates, w_up_sb, w_up_rw, w_o, g_ffn, w_ffn_in, w_ffn_out, g_final)


def kernel(x_prompt, x_sample, cache_k, cache_v, page_table, state_wkv, state_shift, norm_mix_g, w_in, sb_bias, b_gate, mu_shift, w0, w_decay_up, a0, w_a_up, w_g_up, k_k, k_a, r_k, lnx_w, lnx_b, w_up_sb, w_up_rw, w_o, norm_ffn_g, w_ffn_in, w_ffn_out, norm_final_g):
    depth = w_in.shape[0]
    assert depth == 1
    bp, tp, _ = x_prompt.shape
    bs, ts, _ = x_sample.shape
    l = 0

    w = w_in[l]
    att_cols = 3 * WIDTH
    wq = w[:, 0:WIDTH].astype(BF16)
    wk = w[:, WIDTH:2 * WIDTH].astype(BF16)
    wv = w[:, 2 * WIDTH:att_cols].astype(BF16)
    wrw = _pad_rw(w[:, att_cols:att_cols + RW_COLS]).astype(BF16)
    wg = w[:, att_cols + RW_COLS:].astype(BF16)
    bg = b_gate[l][None, :]
    g_mix = norm_mix_g[l][None, :]
    mu = _pad_rw(mu_shift[l])[None, :]
    pad_rows = lambda m, n: jnp.pad(m, ((0, n - m.shape[0]), (0, 0))).astype(BF16)
    wd_up = pad_rows(w_decay_up[l], WD_SLOT)
    wa_up = pad_rows(w_a_up[l], AD_SLOT)
    wg_up = pad_rows(w_g_up[l], GD_SLOT)
    vec = lambda p: p.reshape(1, WIDTH)
    rw_vecs = (vec(w0[l]), vec(a0[l]), vec(k_k[l]), vec(k_a[l]), vec(r_k[l]), vec(lnx_w[l]), vec(lnx_b[l]))
    ch = jnp.arange(WIDTH, dtype=jnp.int32) // HEAD_DIM
    e_mat = (ch[:, None] == ch[None, :]).astype(BF16)
    ffn_w = (w_up_sb[l].astype(BF16), w_up_rw[l].astype(BF16), w_o[l].astype(BF16), norm_ffn_g[l][None, :],
             w_ffn_in[l].astype(BF16), w_ffn_out[l].astype(BF16), norm_final_g[None, :])
    bias = sb_bias[l]

    def group(x, tm):
        n = x.shape[0] * x.shape[1]
        return _inproj(x.reshape(n, D_MODEL), g_mix, wq, wk, wv, wrw, wg, bg, tm=tm)

    q, k, v, kb, vb, rw, gates = group(x_prompt, 512)
    sh = (bp, tp, WIDTH)
    o_sb = _sb_prompt(q.reshape(sh), kb.reshape(sh), vb.reshape(sh), bias, tq=SB_BLOCK)
    rw3 = rw.reshape(bp, tp, RW_PAD)
    o_rw, wkv_p = _rwkv(rw3, jnp.zeros((bp, 1, RW_PAD), F32), jnp.zeros((bp, N_HEADS, HEAD_DIM, HEAD_DIM), F32),
                        mu, *rw_vecs, wd_up, wa_up, wg_up, e_mat, tb=RWKV_CHUNK, nb=RWKV_PROMPT_SEQS)
    y_p = _merge_ffn(x_prompt.reshape(bp * tp, D_MODEL), o_sb.reshape(bp * tp, WIDTH), o_rw.reshape(bp * tp, WIDTH),
                     gates, *ffn_w, tm=512)
    k_prompt = k.reshape(1, bp, tp, N_HEADS, HEAD_DIM)
    v_prompt = v.reshape(1, bp, tp, N_HEADS, HEAD_DIM)
    shift_p = _unpad_rw(rw3[:, -1, :])[None]

    q, k, v, kb, vb, rw, gates = group(x_sample, 512)
    sh = (bs, ts, WIDTH)
    n_phys, page = cache_k.shape[1], cache_k.shape[2]
    bias_rows = jnp.repeat(bias, ts)[:, None]
    pages_t = lambda c: jnp.transpose(c, (0, 1, 3, 4, 2)).reshape(depth * n_phys, WIDTH, page)
    o_sb = _sb_sample(q.reshape(sh), k.reshape(sh), v.reshape(sh), pages_t(cache_k), pages_t(cache_v),
                      page_table + l * n_phys, bias_rows)
    rw3 = rw.reshape(bs, ts, RW_PAD)
    o_rw, wkv_s = _rwkv(rw3, _pad_rw(state_shift[l])[:, None, :], state_wkv[l],
                        mu, *rw_vecs, wd_up, wa_up, wg_up, e_mat, tb=ts, nb=RWKV_SAMPLE_SEQS)
    y_s = _merge_ffn(x_sample.reshape(bs * ts, D_MODEL), o_sb.reshape(bs * ts, WIDTH), o_rw.reshape(bs * ts, WIDTH),
                     gates, *ffn_w, tm=512)
    k_sample = k.reshape(1, bs, ts, N_HEADS, HEAD_DIM)
    v_sample = v.reshape(1, bs, ts, N_HEADS, HEAD_DIM)
    shift_s = _unpad_rw(rw3[:, -1, :])[None]

    return (y_p.reshape(bp, tp, D_MODEL), y_s.reshape(bs, ts, D_MODEL),
            k_prompt, v_prompt, wkv_p[None], shift_p,
            k_sample, v_sample, wkv_s[None], shift_s)
```

```python
import functools

import jax
import jax.numpy as jnp
from jax import lax
from jax.experimental import pallas as pl
from jax.experimental.pallas import tpu as pltpu

F32 = jnp.float32
BF16 = jnp.bfloat16

D_MODEL = 1024
HEAD_DIM = 64
N_HEADS = 8
WIDTH = N_HEADS * HEAD_DIM
DECAY_LORA = 64
AAA_LORA = 64
GATE_LORA = 160
RW_COLS = 3 * WIDTH + DECAY_LORA + AAA_LORA + GATE_LORA
D_FF = 2816
RMS_EPS = 1e-6
GN_EPS = 64e-5
LOG2_E = 1.4426950408889634

LANES = 128
RW_R, RW_K, RW_V = 0, WIDTH, 2 * WIDTH
RW_WD, RW_AD, RW_GD = 3 * WIDTH, 3 * WIDTH + LANES, 3 * WIDTH + 2 * LANES
RW_PAD = 3 * WIDTH + 4 * LANES
WD_SLOT, AD_SLOT, GD_SLOT = LANES, LANES, 2 * LANES

VMEM_LIMIT_BYTES = 56 * 1024 * 1024

RWKV_CHUNK = 64
RWKV_SUBBLOCK = 16
RWKV_PROMPT_SEQS = 2
RWKV_SAMPLE_SEQS = 8
SB_QUERY_BLOCK = 512
SB_KEY_BLOCK = 256
SAMPLE_PAGES_PER_STEP = 16


def _dot(a, b):
    return jnp.dot(a, b, preferred_element_type=F32)


def _dot_nt(a, b):
    return lax.dot_general(a, b, (((1,), (1,)), ((), ())), preferred_element_type=F32)


def _dot_tn(a, b):
    return lax.dot_general(a, b, (((0,), (0,)), ((), ())), preferred_element_type=F32)


def _split(x):
    hi = x.astype(BF16)
    lo = (x - hi.astype(F32)).astype(BF16)
    return hi, lo


def _dot3(a, b):
    a_hi, a_lo = _split(a)
    b_hi, b_lo = _split(b)
    return _dot(a_hi, b_hi) + _dot(a_lo, b_hi) + _dot(a_hi, b_lo)


def _softplus(z):
    return jnp.maximum(z, 0.0) + jnp.log(1.0 + jnp.exp2(jnp.abs(z) * (-LOG2_E)))


def _sigmoid(x):
    return 1.0 / (1.0 + jnp.exp(-x))


def _const_spec(shape):
    nd = len(shape)
    return pl.BlockSpec(shape, lambda *_: (0,) * nd, pipeline_mode=pl.Buffered(1))


def _pad_rw(x):
    def slot(lo, width, size):
        part = x[..., lo:lo + width]
        pad = [(0, 0)] * (x.ndim - 1) + [(0, size - width)]
        return jnp.pad(part, pad)
    o = 3 * WIDTH
    return jnp.concatenate([
        x[..., :o],
        slot(o, DECAY_LORA, WD_SLOT),
        slot(o + DECAY_LORA, AAA_LORA, AD_SLOT),
        slot(o + DECAY_LORA + AAA_LORA, GATE_LORA, GD_SLOT),
    ], axis=-1)


def _unpad_rw(x):
    return jnp.concatenate([
        x[..., :3 * WIDTH],
        x[..., RW_WD:RW_WD + DECAY_LORA],
        x[..., RW_AD:RW_AD + AAA_LORA],
        x[..., RW_GD:RW_GD + GATE_LORA],
    ], axis=-1)


def _inproj_kernel(x_ref, g_ref, wq_ref, wk_ref, wv_ref, wrw_ref, wg_ref, bg_ref,
                   q_ref, k_ref, v_ref, kb_ref, vb_ref, rw_ref, gate_ref):
    x = x_ref[...]
    ms = jnp.mean(x * x, axis=-1, keepdims=True)
    h = (x * lax.rsqrt(ms + RMS_EPS) * g_ref[...]).astype(BF16)
    q_ref[...] = _dot(h, wq_ref[...]) * (HEAD_DIM ** -0.5)
    k = _dot(h, wk_ref[...])
    k_ref[...] = k
    kb_ref[...] = k.astype(BF16)
    v = _dot(h, wv_ref[...])
    v_ref[...] = v
    vb_ref[...] = v.astype(BF16)
    rw_ref[...] = _dot(h, wrw_ref[...])
    gate_ref[...] = _sigmoid(_dot(h, wg_ref[...]) + bg_ref[...])


def _inproj(x, g, wq, wk, wv, wrw, wg, bg, *, tm):
    n = x.shape[0]
    row = lambda w: pl.BlockSpec((tm, w), lambda i: (i, 0))
    return pl.pallas_call(
        _inproj_kernel,
        grid=(n // tm,),
        in_specs=[row(D_MODEL), _const_spec((1, D_MODEL)),
                  _const_spec((D_MODEL, WIDTH)), _const_spec((D_MODEL, WIDTH)), _const_spec((D_MODEL, WIDTH)),
                  _const_spec((D_MODEL, RW_PAD)), _const_spec((D_MODEL, 2 * D_MODEL)),
                  _const_spec((1, 2 * D_MODEL))],
        out_specs=[row(WIDTH), row(WIDTH), row(WIDTH), row(WIDTH), row(WIDTH), row(RW_PAD), row(2 * D_MODEL)],
        out_shape=[jax.ShapeDtypeStruct((n, WIDTH), F32),
                   jax.ShapeDtypeStruct((n, WIDTH), F32),
                   jax.ShapeDtypeStruct((n, WIDTH), F32),
                   jax.ShapeDtypeStruct((n, WIDTH), BF16),
                   jax.ShapeDtypeStruct((n, WIDTH), BF16),
                   jax.ShapeDtypeStruct((n, RW_PAD), F32),
                   jax.ShapeDtypeStruct((n, 2 * D_MODEL), F32)],
        compiler_params=pltpu.CompilerParams(dimension_semantics=("parallel",),
                                             vmem_limit_bytes=VMEM_LIMIT_BYTES),
        name="inproj",
    )(x, g, wq, wk, wv, wrw, wg, bg)


def _sb_prompt_kernel(bias_ref, q_ref, k_ref, v_ref, o_ref, acc_ref, d_scr, sp_scr, att_scr, *, tq, tk):
    pair = pl.program_id(1)
    qi = pl.program_id(2)
    lane = lax.broadcasted_iota(jnp.int32, (1, LANES), 1)
    q = q_ref[...]
    qh = (jnp.where(lane < HEAD_DIM, q, 0.0).astype(BF16), jnp.where(lane >= HEAD_DIM, q, 0.0).astype(BF16))
    bias = (bias_ref[2 * pair], bias_ref[2 * pair + 1])
    tri_r = lax.broadcasted_iota(jnp.int32, (tk, tk), 0)
    tri_c = lax.broadcasted_iota(jnp.int32, (tk, tk), 1)
    tri = (tri_r > tri_c).astype(BF16)
    r = lax.broadcasted_iota(jnp.int32, (tq, tk), 0)
    c = lax.broadcasted_iota(jnp.int32, (tq, tk), 1)
    acc_ref[...] = jnp.zeros_like(acc_ref)
    n_diag = tq // tk
    n_blocks = n_diag * (qi + 1)

    def key_block(p):
        start = pl.multiple_of(jnp.clip(n_blocks - 1 - p, 0, n_blocks - 1) * tk, tk)
        return pl.ds(start, tk)

    def scores(p, visible):
        k = k_ref[key_block(p), :]
        for h in range(2):
            z = _dot_nt(qh[h], k) + bias[h]
            if visible is not None:
                z = jnp.where(visible, z, -1e30)
            sp = _softplus(z)
            d_scr[h] = z - sp
            sp_scr[h] = sp.astype(BF16)

    def weights(carry):
        out = []
        for h in range(2):
            sp = sp_scr[h]
            tail = _dot(sp, tri) + carry[h]
            att_scr[h] = jnp.exp(d_scr[h] - tail).astype(BF16)
            out.append(tail[:, 0:1] + sp[:, 0:1].astype(F32))
        return tuple(out)

    def values(p):
        v = v_ref[key_block(p), :]
        for h in range(2):
            acc_ref[h] += _dot(att_scr[h], v)

    def trip(it, carry, visible=None):
        values(it - 2)
        carry = weights(carry)
        scores(it, visible)
        return carry

    def visible(p):
        return c + (n_diag - 1 - p) * tk < r

    scores(0, visible(0))
    zero = jnp.zeros((tq, 1), F32)
    carry = (zero, zero)
    if n_diag == 1:
        att_scr[...] = jnp.zeros_like(att_scr)
    else:
        carry = weights(carry)
        scores(1, visible(1))
        for it in range(2, n_diag):
            carry = trip(it, carry, visible(it))
    lax.fori_loop(n_diag, n_blocks + 2, trip, carry)
    o_ref[...] = jnp.where(lane < HEAD_DIM, acc_ref[0], acc_ref[1]).astype(o_ref.dtype)


def _sb_prompt(q, kb, vb, bias, *, tq, tk):
    b, t, _ = q.shape
    assert tq % tk == 0 and t % tq == 0
    kv_spec = pl.BlockSpec((None, t, LANES), lambda bi, p, i: (bi, 0, p))
    return pl.pallas_call(
        functools.partial(_sb_prompt_kernel, tq=tq, tk=tk),
        grid=(b, WIDTH // LANES, t // tq),
        in_specs=[pl.BlockSpec(memory_space=pltpu.SMEM),
                  pl.BlockSpec((None, tq, LANES), lambda bi, p, i: (bi, i, p)),
                  kv_spec, kv_spec],
        out_specs=pl.BlockSpec((None, tq, LANES), lambda bi, p, i: (bi, i, p)),
        out_shape=jax.ShapeDtypeStruct((b, t, WIDTH), BF16),
        scratch_shapes=[pltpu.VMEM((2, tq, LANES), F32), pltpu.VMEM((2, tq, tk), F32),
                        pltpu.VMEM((2, tq, tk), BF16), pltpu.VMEM((2, tq, tk), BF16)],
        compiler_params=pltpu.CompilerParams(dimension_semantics=("parallel", "parallel", "arbitrary"),
                                             vmem_limit_bytes=VMEM_LIMIT_BYTES),
        name="sb_prompt",
    )(bias, q, kb, vb)


def _sb_sample_kernel(pt_ref, q_ref, kn_ref, vn_ref, bias_ref, *rest, n_pages_step, t_new, page):
    kt_refs = rest[:n_pages_step]
    vt_refs = rest[n_pages_step:2 * n_pages_step]
    o_ref, qbd_ref, c_ref, acc_ref = rest[2 * n_pages_step:]
    del pt_ref
    j = pl.program_id(1)
    rows = N_HEADS * t_new
    r = lax.broadcasted_iota(jnp.int32, (page, page), 0)
    c = lax.broadcasted_iota(jnp.int32, (page, page), 1)
    tri = (r > c).astype(BF16)
    bias = bias_ref[...]

    def blocks(zs, avs):
        zs = [z + bias for z in zs]
        sps = [_softplus(z) for z in zs]
        sps_b = [sp.astype(BF16) for sp in sps]
        incl = [sp + _dot(sp_b, tri) for sp, sp_b in zip(sps, sps_b)]
        totals = [jnp.broadcast_to(t[:, 0:1], t.shape) for t in incl]
        run = c_ref[...]
        acc = acc_ref[...]
        for z, t, tot, av in zip(zs, incl, totals, avs):
            acc = acc + av(jnp.exp(z - (t + run)).astype(BF16))
            run = run + tot
        acc_ref[...] = acc
        c_ref[...] = run

    @pl.when(j == 0)
    def _():
        q = q_ref[...]
        qt = jnp.concatenate([q] * N_HEADS, axis=0)
        rr = lax.broadcasted_iota(jnp.int32, (rows, WIDTH), 0)
        cc = lax.broadcasted_iota(jnp.int32, (rows, WIDTH), 1)
        qbd_ref[...] = jnp.where(rr // t_new == cc // HEAD_DIM, qt, 0.0).astype(BF16)
        c_ref[...] = jnp.zeros_like(c_ref)
        acc_ref[...] = jnp.zeros_like(acc_ref)
        pad = jnp.zeros((page - t_new, WIDTH), F32)
        kn = jnp.concatenate([kn_ref[...], pad], axis=0).astype(BF16)
        vn = jnp.concatenate([vn_ref[...], pad], axis=0).astype(BF16)
        qrow = lax.broadcasted_iota(jnp.int32, (rows, page), 0) % t_new
        key = lax.broadcasted_iota(jnp.int32, (rows, page), 1)
        z = jnp.where(key < qrow, _dot_nt(qbd_ref[...], kn), -1e30)
        blocks([z], [lambda att: _dot(att, vn)])

    qbd = qbd_ref[...]
    blocks([_dot(qbd, kt_ref[...].astype(BF16)) for kt_ref in kt_refs],
           [lambda att, vt_ref=vt_ref: _dot_nt(att, vt_ref[...].astype(BF16)) for vt_ref in vt_refs])

    @pl.when(j == pl.num_programs(1) - 1)
    def _():
        lane_head = lax.broadcasted_iota(jnp.int32, (t_new, WIDTH), 1) // HEAD_DIM
        out = jnp.zeros((t_new, WIDTH), F32)
        for h in range(N_HEADS):
            out = out + jnp.where(lane_head == h, acc_ref[h * t_new:(h + 1) * t_new, :], 0.0)
        o_ref[...] = out.astype(o_ref.dtype)


def _sb_sample(q, k_new, v_new, cache_kt, cache_vt, page_table, bias_rows):
    s, t_new, _ = q.shape
    n_pages = page_table.shape[1]
    page = cache_kt.shape[2]
    g_pages = SAMPLE_PAGES_PER_STEP
    assert n_pages % g_pages == 0
    rows = N_HEADS * t_new

    def page_spec(g):
        return pl.BlockSpec((None, WIDTH, page),
                            lambda n, j, pt: (pt[n, n_pages - 1 - (j * g_pages + g)], 0, 0))

    seq_spec = pl.BlockSpec((None, t_new, WIDTH), lambda n, j, pt: (n, 0, 0))
    grid_spec = pltpu.PrefetchScalarGridSpec(
        num_scalar_prefetch=1,
        grid=(s, n_pages // g_pages),
        in_specs=[seq_spec, seq_spec, seq_spec,
                  pl.BlockSpec((rows, 1), lambda n, j, pt: (0, 0))]
                 + [page_spec(g) for g in range(g_pages)] * 2,
        out_specs=seq_spec,
        scratch_shapes=[pltpu.VMEM((rows, WIDTH), BF16),
                        pltpu.VMEM((rows, page), F32),
                        pltpu.VMEM((rows, WIDTH), F32)],
    )
    return pl.pallas_call(
        functools.partial(_sb_sample_kernel, n_pages_step=g_pages, t_new=t_new, page=page),
        grid_spec=grid_spec,
        out_shape=jax.ShapeDtypeStruct((s, t_new, WIDTH), BF16),
        compiler_params=pltpu.CompilerParams(dimension_semantics=("parallel", "arbitrary"),
                                             vmem_limit_bytes=VMEM_LIMIT_BYTES),
        name="sb_sample",
    )(page_table, q, k_new, v_new, bias_rows, *([cache_kt] * g_pages), *([cache_vt] * g_pages))


def _rwkv_kernel(rw_ref, prev0_ref, s0_ref, mu_ref, w0_ref, a0_ref, kk_ref, ka_ref, rk_ref, lnw_ref, lnb_ref,
                 wd_ref, wa_ref, wg_ref, e_ref,
                 o_ref, sout_ref,
                 sbd_scr, prev_scr, *, nb, tb, chunk):
    assert chunk == HEAD_DIM
    j = pl.program_id(1)
    n_pairs = WIDTH // LANES
    low = lax.broadcasted_iota(jnp.int32, (1, LANES), 1) < HEAD_DIM

    @pl.when(j == 0)
    def _():
        prev_scr[...] = prev0_ref[...]
        zero = jnp.zeros((HEAD_DIM, HEAD_DIM), F32)
        for b in range(nb):
            for p in range(n_pairs):
                top = jnp.concatenate([s0_ref[b, 2 * p], zero], axis=1)
                bot = jnp.concatenate([zero, s0_ref[b, 2 * p + 1]], axis=1)
                sbd_scr[b, p] = jnp.concatenate([top, bot], axis=0)

    e_mat = e_ref[...]
    row = lax.broadcasted_iota(jnp.int32, (chunk, 1), 0)
    bf = lambda x: x.astype(BF16)

    def head_sum(x):
        hi, lo = _split(x)
        return _dot(hi, e_mat) + _dot(lo, e_mat)

    def prepare(b):
        rw = rw_ref[b]
        if tb < chunk:
            rw = jnp.concatenate([rw, jnp.zeros((chunk - tb, RW_PAD), F32)], axis=0)
        prev = jnp.where(row == 0, prev_scr[b], pltpu.roll(rw, 1, axis=0))
        xs = rw + (prev - rw) * mu_ref[...]
        prev_scr[b] = rw[tb - 1:tb, :]

        r = xs[:, RW_R:RW_R + WIDTH]
        k = xs[:, RW_K:RW_K + WIDTH]
        v = xs[:, RW_V:RW_V + WIDTH]
        wd = xs[:, RW_WD:RW_WD + WD_SLOT]
        ad = xs[:, RW_AD:RW_AD + AD_SLOT]
        gd = xs[:, RW_GD:RW_GD + GD_SLOT]

        x_w = w0_ref[...] + _dot(bf(jnp.tanh(wd)), wd_ref[...])
        w_log = -_softplus(-x_w) - 0.5
        lw = -jnp.exp(w_log)
        a_sig = _sigmoid(a0_ref[...] + _dot(bf(ad), wa_ref[...]))
        g = _dot(bf(_sigmoid(gd)), wg_ref[...])
        kkr = k * kk_ref[...]
        kk = kkr / jnp.maximum(jnp.sqrt(head_sum(kkr * kkr)), 1e-12)
        kmod = k * (1.0 + (a_sig - 1.0) * ka_ref[...])
        if tb < chunk:
            valid = row < tb
            lw = jnp.where(valid, lw, 0.0)
            kk = jnp.where(valid, kk, 0.0)
            kmod = jnp.where(valid, kmod, 0.0)

        lc = lw
        shift = 1
        while shift < chunk:
            lc = lc + jnp.where(row >= shift, pltpu.roll(lc, shift, axis=0), 0.0)
            shift *= 2
        lprev = lc - lw
        lref = lc[chunk // 2 - 1:chunk // 2, :]
        lend = lc[chunk - 1:chunk, :]
        e_b = jnp.exp(lref - lc)
        e_e = jnp.exp(lend - lc)
        b_vec = kk * a_sig
        return dict(
            r=r, v=v, kmod=kmod, g=g,
            a_hat=-kk * jnp.exp(lprev - lref), a_start=-kk * jnp.exp(lprev),
            r_hat=r * jnp.exp(lc - lref), r_start=r * jnp.exp(lc),
            b_hat=b_vec * e_b, k_hat=kmod * e_b, b_til=b_vec * e_e, k_til=kmod * e_e,
            dec_end=jnp.exp(lend))

    ss = lax.broadcasted_iota(jnp.int32, (chunk, LANES), 0)
    tt = lax.broadcasted_iota(jnp.int32, (chunk, LANES), 1) % HEAD_DIM
    same_head = (lax.broadcasted_iota(jnp.int32, (LANES, LANES), 0) // HEAD_DIM
                 == lax.broadcasted_iota(jnp.int32, (LANES, LANES), 1) // HEAD_DIM)

    def stack2(x):
        return jnp.concatenate([jnp.where(low, x, 0.0), jnp.where(low, 0.0, x)], axis=0)

    seqs = [prepare(b) for b in range(nb)]
    chains = [(b, p) for b in range(nb) for p in range(n_pairs)]
    part = lambda b, p, name: seqs[b][name][:, p * LANES:(p + 1) * LANES]

    sub = RWKV_SUBBLOCK
    n_sub = -(-min(chunk, tb) // sub)
    same_sub = ss // sub == tt // sub
    each = lambda f: [f(i, b, p) for i, (b, p) in enumerate(chains)]
    sbds = each(lambda i, b, p: sbd_scr[b, p])
    ahs = each(lambda i, b, p: bf(part(b, p, "a_hat")))
    bh2 = each(lambda i, b, p: bf(stack2(part(b, p, "b_hat"))))
    kh2 = each(lambda i, b, p: bf(stack2(part(b, p, "k_hat"))))
    v2 = each(lambda i, b, p: bf(stack2(part(b, p, "v"))))
    mts = each(lambda i, b, p: _dot_nt(bf(part(b, p, "b_hat")), bf(stack2(part(b, p, "a_hat")))))
    ms = each(lambda i, b, p: jnp.where(tt < ss, _dot_nt(ahs[i], bh2[i]), 0.0))
    m_aks = each(lambda i, b, p: jnp.where(tt < ss, _dot_nt(ahs[i], kh2[i]), 0.0))
    rhs0 = each(lambda i, b, p: _dot_nt(bf(part(b, p, "a_start")), bf(sbds[i])))
    rhss = each(lambda i, b, p: rhs0[i] + _dot(bf(m_aks[i]), v2[i]))
    def pack(x):
        d = jnp.where(same_sub, x, 0.0)
        out = d[0:sub]
        for blk in range(1, n_sub):
            out = out + d[blk * sub:(blk + 1) * sub]
        return out

    mtps = [pack(jnp.where(ss < tt, mt, 0.0)) for mt in mts]
    eps = [pack(m) for m in ms]
    prow = lax.broadcasted_iota(jnp.int32, (sub, 1), 0)
    group0 = lax.broadcasted_iota(jnp.int32, (sub, LANES), 1) // sub * sub
    for tau in range(1, min(sub, tb)):
        for i in range(len(chains)):
            coef = jnp.take_along_axis(mtps[i], group0 + tau, axis=1)
            contrib = jnp.sum(coef * eps[i], axis=0, keepdims=True)
            eps[i] = jnp.where(prow == tau, eps[i] + contrib, eps[i])

    def unpack(x):
        blk_of_lane = lax.broadcasted_iota(jnp.int32, (sub, LANES), 1) // sub % (HEAD_DIM // sub)
        rows = [jnp.where(blk_of_lane == blk, x, 0.0) if blk < n_sub else jnp.zeros((sub, LANES), F32)
                for blk in range(chunk // sub)]
        return jnp.concatenate(rows, axis=0)

    es = [unpack(x) for x in eps]
    size = sub
    while size < min(chunk, tb):
        coupled = ((ss // size) % 2 == 1) & (tt // size == ss // size - 1)
        ns = [jnp.where(coupled, m, 0.0) for m in ms]
        qs = each(lambda i, b, p: ns[i] + _dot3(es[i], stack2(ns[i])))
        es = each(lambda i, b, p: es[i] + qs[i] + _dot3(qs[i], stack2(es[i])))
        size *= 2
    us = each(lambda i, b, p: rhss[i] + _dot3(es[i], stack2(rhss[i])))

    rhs_b = each(lambda i, b, p: bf(part(b, p, "r_hat")))
    m_rbs = each(lambda i, b, p: jnp.where(tt <= ss, _dot_nt(rhs_b[i], bh2[i]), 0.0))
    m_rks = each(lambda i, b, p: jnp.where(tt <= ss, _dot_nt(rhs_b[i], kh2[i]), 0.0))
    y0 = each(lambda i, b, p: _dot_nt(bf(part(b, p, "r_start")), bf(sbds[i])))
    ys = each(lambda i, b, p: y0[i] + _dot(bf(jnp.concatenate([m_rbs[i], m_rks[i]], axis=1)),
                                          jnp.concatenate([bf(stack2(us[i])), v2[i]], axis=0)))
    upds = each(lambda i, b, p: _dot_tn(bf(jnp.concatenate([us[i], part(b, p, "v")], axis=0)),
                                        bf(jnp.concatenate([part(b, p, "b_til"), part(b, p, "k_til")], axis=0))))
    y_parts = [[] for _ in range(nb)]
    for i, (b, p) in enumerate(chains):
        y_parts[b].append(ys[i])
        sbd_scr[b, p] = jnp.where(same_head, sbds[i] * part(b, p, "dec_end") + upds[i], 0.0)

    inv_n = 1.0 / HEAD_DIM
    for b in range(nb):
        s = seqs[b]
        y = jnp.concatenate(y_parts[b], axis=-1)
        mean = head_sum(y) * inv_n
        d = y - mean
        var = head_sum(d * d) * inv_n
        yn = d * lax.rsqrt(var + GN_EPS) * lnw_ref[...] + lnb_ref[...]
        bonus = head_sum(s["r"] * s["kmod"] * rk_ref[...]) * s["v"]
        out = (yn + bonus) * s["g"]
        o_ref[b] = out[:tb].astype(o_ref.dtype)

    @pl.when(j == pl.num_programs(1) - 1)
    def _():
        for b in range(nb):
            for p in range(n_pairs):
                sbd = sbd_scr[b, p]
                sout_ref[b, 2 * p] = sbd[:HEAD_DIM, :HEAD_DIM]
                sout_ref[b, 2 * p + 1] = sbd[HEAD_DIM:, HEAD_DIM:]


def _rwkv(rw, prev0, s0, mu, w0, a0, k_k, k_a, r_k, lnx_w, lnx_b, wd_up, wa_up, wg_up, e_mat, *, tb, nb):
    b, t, _ = rw.shape
    chunk = RWKV_CHUNK
    assert t % tb == 0 and tb <= chunk and b % nb == 0
    vec = _const_spec((1, WIDTH))
    state_spec = pl.BlockSpec((nb, N_HEADS, HEAD_DIM, HEAD_DIM), lambda bi, j: (bi, 0, 0, 0))
    return pl.pallas_call(
        functools.partial(_rwkv_kernel, nb=nb, tb=tb, chunk=chunk),
        grid=(b // nb, t // tb),
        in_specs=[pl.BlockSpec((nb, tb, RW_PAD), lambda bi, j: (bi, j, 0)),
                  pl.BlockSpec((nb, 1, RW_PAD), lambda bi, j: (bi, 0, 0)),
                  state_spec,
                  _const_spec((1, RW_PAD)),
                  vec, vec, vec, vec, vec, vec, vec,
                  _const_spec((WD_SLOT, WIDTH)), _const_spec((AD_SLOT, WIDTH)), _const_spec((GD_SLOT, WIDTH)),
                  _const_spec((WIDTH, WIDTH))],
        out_specs=[pl.BlockSpec((nb, tb, WIDTH), lambda bi, j: (bi, j, 0)), state_spec],
        out_shape=[jax.ShapeDtypeStruct((b, t, WIDTH), BF16),
                   jax.ShapeDtypeStruct((b, N_HEADS, HEAD_DIM, HEAD_DIM), F32)],
        scratch_shapes=[pltpu.VMEM((nb, WIDTH // LANES, LANES, LANES), F32),
                        pltpu.VMEM((nb, 1, RW_PAD), F32)],
        compiler_params=pltpu.CompilerParams(dimension_semantics=("parallel", "arbitrary"),
                                             vmem_limit_bytes=VMEM_LIMIT_BYTES),
        name="rwkv",
    )(rw, prev0, s0, mu, w0, a0, k_k, k_a, r_k, lnx_w, lnx_b, wd_up, wa_up, wg_up, e_mat)


def _merge_ffn_kernel(x_ref, osb_ref, orw_ref, gate_ref, wsb_ref, wrw_ref, wo_ref, gffn_ref, win_ref, wout_ref,
                      gfin_ref, y_ref, *, ff_chunk):
    gates = gate_ref[...]
    merged = (gates[:, :D_MODEL] * _dot(osb_ref[...], wsb_ref[...])
              + gates[:, D_MODEL:] * _dot(orw_ref[...], wrw_ref[...]))
    x = x_ref[...] + _dot(merged.astype(BF16), wo_ref[...])
    ms = jnp.mean(x * x, axis=-1, keepdims=True)
    h = (x * lax.rsqrt(ms + RMS_EPS) * gffn_ref[...]).astype(BF16)
    for c in range(D_FF // ff_chunk):
        lo = c * ff_chunk
        gate = _dot(h, win_ref[:, lo:lo + ff_chunk])
        up = _dot(h, win_ref[:, D_FF + lo:D_FF + lo + ff_chunk])
        act = (gate * _sigmoid(gate) * up).astype(BF16)
        x = x + _dot(act, wout_ref[lo:lo + ff_chunk, :])
    ms = jnp.mean(x * x, axis=-1, keepdims=True)
    y_ref[...] = x * lax.rsqrt(ms + RMS_EPS) * gfin_ref[...]


def _merge_ffn(x, o_sb, o_rw, gates, w_up_sb, w_up_rw, w_o, g_ffn, w_ffn_in, w_ffn_out, g_final, *, tm):
    n = x.shape[0]
    row = lambda w: pl.BlockSpec((tm, w), lambda i: (i, 0))
    return pl.pallas_call(
        functools.partial(_merge_ffn_kernel, ff_chunk=D_FF // 2),
        grid=(n // tm,),
        in_specs=[row(D_MODEL), row(WIDTH), row(WIDTH), row(2 * D_MODEL),
                  _const_spec((WIDTH, D_MODEL)), _const_spec((WIDTH, D_MODEL)), _const_spec((D_MODEL, D_MODEL)),
                  _const_spec((1, D_MODEL)), _const_spec((D_MODEL, 2 * D_FF)), _const_spec((D_FF, D_MODEL)),
                  _const_spec((1, D_MODEL))],
        out_specs=row(D_MODEL),
        out_shape=jax.ShapeDtypeStruct((n, D_MODEL), F32),
        compiler_params=pltpu.CompilerParams(dimension_semantics=("parallel",),
                                             vmem_limit_bytes=VMEM_LIMIT_BYTES),
        name="merge_ffn",
    )(x, o_sb, o_rw, gates, w_up_sb, w_up_rw, w_o, g_ffn, w_ffn_in, w_ffn_out, g_final)


def kernel(x_prompt, x_sample, cache_k, cache_v, page_table, state_wkv, state_shift, norm_mix_g, w_in, sb_bias, b_gate, mu_shift, w0, w_decay_up, a0, w_a_up, w_g_up, k_k, k_a, r_k, lnx_w, lnx_b, w_up_sb, w_up_rw, w_o, norm_ffn_g, w_ffn_in, w_ffn_out, norm_final_g):
    depth = w_in.shape[0]
    assert depth == 1
    bp, tp, _ = x_prompt.shape
    bs, ts, _ = x_sample.shape
    l = 0

    w = w_in[l]
    att_cols = 3 * WIDTH
    wq = w[:, 0:WIDTH].astype(BF16)
    wk = w[:, WIDTH:2 * WIDTH].astype(BF16)
    wv = w[:, 2 * WIDTH:att_cols].astype(BF16)
    wrw = _pad_rw(w[:, att_cols:att_cols + RW_COLS]).astype(BF16)
    wg = w[:, att_cols + RW_COLS:].astype(BF16)
    bg = b_gate[l][None, :]
    g_mix = norm_mix_g[l][None, :]
    mu = _pad_rw(mu_shift[l])[None, :]
    pad_rows = lambda m, n: jnp.pad(m, ((0, n - m.shape[0]), (0, 0))).astype(BF16)
    wd_up = pad_rows(w_decay_up[l], WD_SLOT)
    wa_up = pad_rows(w_a_up[l], AD_SLOT)
    wg_up = pad_rows(w_g_up[l], GD_SLOT)
    vec = lambda p: p.reshape(1, WIDTH)
    rw_vecs = (vec(w0[l]), vec(a0[l]), vec(k_k[l]), vec(k_a[l]), vec(r_k[l]), vec(lnx_w[l]), vec(lnx_b[l]))
    ch = jnp.arange(WIDTH, dtype=jnp.int32) // HEAD_DIM
    e_mat = (ch[:, None] == ch[None, :]).astype(BF16)
    ffn_w = (w_up_sb[l].astype(BF16), w_up_rw[l].astype(BF16), w_o[l].astype(BF16), norm_ffn_g[l][None, :],
             w_ffn_in[l].astype(BF16), w_ffn_out[l].astype(BF16), norm_final_g[None, :])
    bias = sb_bias[l]

    def group(x, tm):
        n = x.shape[0] * x.shape[1]
        return _inproj(x.reshape(n, D_MODEL), g_mix, wq, wk, wv, wrw, wg, bg, tm=tm)

    q, k, v, kb, vb, rw, gates = group(x_prompt, 512)
    sh = (bp, tp, WIDTH)
    o_sb = _sb_prompt(q.reshape(sh), kb.reshape(sh), vb.reshape(sh), bias, tq=SB_QUERY_BLOCK, tk=SB_KEY_BLOCK)
    rw3 = rw.reshape(bp, tp, RW_PAD)
    o_rw, wkv_p = _rwkv(rw3, jnp.zeros((bp, 1, RW_PAD), F32), jnp.zeros((bp, N_HEADS, HEAD_DIM, HEAD_DIM), F32),
                        mu, *rw_vecs, wd_up, wa_up, wg_up, e_mat, tb=RWKV_CHUNK, nb=RWKV_PROMPT_SEQS)
    y_p = _merge_ffn(x_prompt.reshape(bp * tp, D_MODEL), o_sb.reshape(bp * tp, WIDTH), o_rw.reshape(bp * tp, WIDTH),
                     gates, *ffn_w, tm=512)
    k_prompt = k.reshape(1, bp, tp, N_HEADS, HEAD_DIM)
    v_prompt = v.reshape(1, bp, tp, N_HEADS, HEAD_DIM)
    shift_p = _unpad_rw(rw3[:, -1, :])[None]

    q, k, v, kb, vb, rw, gates = group(x_sample, 512)
    sh = (bs, ts, WIDTH)
    n_phys, page = cache_k.shape[1], cache_k.shape[2]
    bias_rows = jnp.repeat(bias, ts)[:, None]
    pages_t = lambda c: jnp.transpose(c, (0, 1, 3, 4, 2)).reshape(depth * n_phys, WIDTH, page)
    o_sb = _sb_sample(q.reshape(sh), k.reshape(sh), v.reshape(sh), pages_t(cache_k), pages_t(cache_v),
                      page_table + l * n_phys, bias_rows)
    rw3 = rw.reshape(bs, ts, RW_PAD)
    o_rw, wkv_s = _rwkv(rw3, _pad_rw(state_shift[l])[:, None, :], state_wkv[l],
                        mu, *rw_vecs, wd_up, wa_up, wg_up, e_mat, tb=ts, nb=RWKV_SAMPLE_SEQS)
    y_s = _merge_ffn(x_sample.reshape(bs * ts, D_MODEL), o_sb.reshape(bs * ts, WIDTH), o_rw.reshape(bs * ts, WIDTH),
                     gates, *ffn_w, tm=512)
    k_sample = k.reshape(1, bs, ts, N_HEADS, HEAD_DIM)
    v_sample = v.reshape(1, bs, ts, N_HEADS, HEAD_DIM)
    shift_s = _unpad_rw(rw3[:, -1, :])[None]

    return (y_p.reshape(bp, tp, D_MODEL), y_s.reshape(bs, ts, D_MODEL),
            k_prompt, v_prompt, wkv_p[None], shift_p,
            k_sample, v_sample, wkv_s[None], shift_s)
```

```python
import functools

import jax
import jax.numpy as jnp
from jax import lax
from jax.experimental import pallas as pl
from jax.experimental.pallas import tpu as pltpu

F32 = jnp.float32
BF16 = jnp.bfloat16

D_MODEL = 1024
HEAD_DIM = 64
N_HEADS = 8
WIDTH = N_HEADS * HEAD_DIM
DECAY_LORA = 64
AAA_LORA = 64
GATE_LORA = 160
RW_COLS = 3 * WIDTH + DECAY_LORA + AAA_LORA + GATE_LORA
D_FF = 2816
RMS_EPS = 1e-6
GN_EPS = 64e-5
LOG2_E = 1.4426950408889634

LANES = 128
RW_R, RW_K, RW_V = 0, WIDTH, 2 * WIDTH
RW_WD, RW_AD, RW_GD = 3 * WIDTH, 3 * WIDTH + LANES, 3 * WIDTH + 2 * LANES
RW_PAD = 3 * WIDTH + 4 * LANES
WD_SLOT, AD_SLOT, GD_SLOT = LANES, LANES, 2 * LANES

VMEM_LIMIT_BYTES = 56 * 1024 * 1024

RWKV_CHUNK = 64
RWKV_SUBBLOCK = 16
RWKV_PROMPT_SEQS = 2
RWKV_SAMPLE_SEQS = 8
SB_QUERY_BLOCK = 512
SB_KEY_BLOCK = 256
SAMPLE_PAGES_PER_STEP = 16


def _dot(a, b):
    return jnp.dot(a, b, preferred_element_type=F32)


def _dot_nt(a, b):
    return lax.dot_general(a, b, (((1,), (1,)), ((), ())), preferred_element_type=F32)


def _dot_tn(a, b):
    return lax.dot_general(a, b, (((0,), (0,)), ((), ())), preferred_element_type=F32)


def _split(x):
    hi = x.astype(BF16)
    lo = (x - hi.astype(F32)).astype(BF16)
    return hi, lo


def _dot3(a, b):
    a_hi, a_lo = _split(a)
    b_hi, b_lo = _split(b)
    return _dot(a_hi, b_hi) + _dot(a_lo, b_hi) + _dot(a_hi, b_lo)


def _softplus(z):
    return jnp.maximum(z, 0.0) + jnp.log(1.0 + jnp.exp2(jnp.abs(z) * (-LOG2_E)))


def _sigmoid(x):
    return 1.0 / (1.0 + jnp.exp(-x))


def _const_spec(shape):
    nd = len(shape)
    return pl.BlockSpec(shape, lambda *_: (0,) * nd, pipeline_mode=pl.Buffered(1))


def _pad_rw(x):
    def slot(lo, width, size):
        part = x[..., lo:lo + width]
        pad = [(0, 0)] * (x.ndim - 1) + [(0, size - width)]
        return jnp.pad(part, pad)
    o = 3 * WIDTH
    return jnp.concatenate([
        x[..., :o],
        slot(o, DECAY_LORA, WD_SLOT),
        slot(o + DECAY_LORA, AAA_LORA, AD_SLOT),
        slot(o + DECAY_LORA + AAA_LORA, GATE_LORA, GD_SLOT),
    ], axis=-1)


def _unpad_rw(x):
    return jnp.concatenate([
        x[..., :3 * WIDTH],
        x[..., RW_WD:RW_WD + DECAY_LORA],
        x[..., RW_AD:RW_AD + AAA_LORA],
        x[..., RW_GD:RW_GD + GATE_LORA],
    ], axis=-1)


def _inproj_kernel(x_ref, g_ref, wq_ref, wk_ref, wv_ref, wrw_ref, wg_ref, bg_ref,
                   q_ref, k_ref, v_ref, kb_ref, vb_ref, rw_ref, gate_ref, *, channel_major_kv):
    x = x_ref[...]
    ms = jnp.mean(x * x, axis=-1, keepdims=True)
    h = (x * lax.rsqrt(ms + RMS_EPS) * g_ref[...]).astype(BF16)
    q_ref[...] = _dot(h, wq_ref[...]) * (HEAD_DIM ** -0.5)
    proj = (lambda w_ref: _dot_nt(w_ref[...], h)) if channel_major_kv else (lambda w_ref: _dot(h, w_ref[...]))
    k = proj(wk_ref)
    k_ref[...] = k
    kb_ref[...] = k.astype(BF16)
    v = proj(wv_ref)
    v_ref[...] = v
    vb_ref[...] = v.astype(BF16)
    rw_ref[...] = _dot(h, wrw_ref[...])
    gate_ref[...] = _sigmoid(_dot(h, wg_ref[...]) + bg_ref[...])


def _inproj(x, g, wq, wk, wv, wrw, wg, bg, *, tm, seq_len=None):
    n = x.shape[0]
    row = lambda w: pl.BlockSpec((tm, w), lambda i: (i, 0))
    if seq_len is None:
        kv_spec, kv_shape, w_kv = row(WIDTH), (n, WIDTH), _const_spec((D_MODEL, WIDTH))
    else:
        assert seq_len % tm == 0
        tiles = seq_len // tm
        kv_spec = pl.BlockSpec((None, WIDTH, tm), lambda i: (i // tiles, 0, i % tiles))
        kv_shape, w_kv = (n // seq_len, WIDTH, seq_len), _const_spec((WIDTH, D_MODEL))
    return pl.pallas_call(
        functools.partial(_inproj_kernel, channel_major_kv=seq_len is not None),
        grid=(n // tm,),
        in_specs=[row(D_MODEL), _const_spec((1, D_MODEL)),
                  _const_spec((D_MODEL, WIDTH)), w_kv, w_kv,
                  _const_spec((D_MODEL, RW_PAD)), _const_spec((D_MODEL, 2 * D_MODEL)),
                  _const_spec((1, 2 * D_MODEL))],
        out_specs=[row(WIDTH), kv_spec, kv_spec, kv_spec, kv_spec, row(RW_PAD), row(2 * D_MODEL)],
        out_shape=[jax.ShapeDtypeStruct((n, WIDTH), F32),
                   jax.ShapeDtypeStruct(kv_shape, F32),
                   jax.ShapeDtypeStruct(kv_shape, F32),
                   jax.ShapeDtypeStruct(kv_shape, BF16),
                   jax.ShapeDtypeStruct(kv_shape, BF16),
                   jax.ShapeDtypeStruct((n, RW_PAD), F32),
                   jax.ShapeDtypeStruct((n, 2 * D_MODEL), F32)],
        compiler_params=pltpu.CompilerParams(dimension_semantics=("parallel",),
                                             vmem_limit_bytes=VMEM_LIMIT_BYTES),
        name="inproj",
    )(x, g, wq, wk, wv, wrw, wg, bg)


def _sb_prompt_kernel(bias_ref, q_ref, k_ref, v_ref, o_ref, acc_ref, d_scr, sp_scr, att_scr, *, tq, tk):
    pair = pl.program_id(1)
    qi = pl.program_id(2)
    lane = lax.broadcasted_iota(jnp.int32, (1, LANES), 1)
    q = q_ref[...]
    qh = (jnp.where(lane < HEAD_DIM, q, 0.0).astype(BF16), jnp.where(lane >= HEAD_DIM, q, 0.0).astype(BF16))
    bias = (bias_ref[2 * pair], bias_ref[2 * pair + 1])
    tri_r = lax.broadcasted_iota(jnp.int32, (tk, tk), 0)
    tri_c = lax.broadcasted_iota(jnp.int32, (tk, tk), 1)
    tri = (tri_r > tri_c).astype(BF16)
    r = lax.broadcasted_iota(jnp.int32, (tq, tk), 0)
    c = lax.broadcasted_iota(jnp.int32, (tq, tk), 1)
    acc_ref[...] = jnp.zeros_like(acc_ref)
    n_diag = tq // tk
    n_blocks = n_diag * (qi + 1)

    def key_block(p):
        start = pl.multiple_of(jnp.clip(n_blocks - 1 - p, 0, n_blocks - 1) * tk, tk)
        return pl.ds(start, tk)

    def scores(p, visible):
        kt = k_ref[:, key_block(p)]
        for h in range(2):
            z = _dot(qh[h], kt) + bias[h]
            if visible is not None:
                z = jnp.where(visible, z, -1e30)
            sp = _softplus(z)
            d_scr[h] = z - sp
            sp_scr[h] = sp.astype(BF16)

    def weights(carry):
        out = []
        for h in range(2):
            sp = sp_scr[h]
            tail = _dot(sp, tri) + carry[h]
            att_scr[h] = jnp.exp(d_scr[h] - tail).astype(BF16)
            out.append(tail[:, 0:1] + sp[:, 0:1].astype(F32))
        return tuple(out)

    def values(p):
        vt = v_ref[:, key_block(p)]
        for h in range(2):
            acc_ref[h] += _dot_nt(att_scr[h], vt)

    def trip(it, carry, visible=None):
        values(it - 2)
        carry = weights(carry)
        scores(it, visible)
        return carry

    def visible(p):
        return c + (n_diag - 1 - p) * tk < r

    scores(0, visible(0))
    zero = jnp.zeros((tq, 1), F32)
    carry = (zero, zero)
    if n_diag == 1:
        att_scr[...] = jnp.zeros_like(att_scr)
    else:
        carry = weights(carry)
        scores(1, visible(1))
        for it in range(2, n_diag):
            carry = trip(it, carry, visible(it))
    carry = lax.fori_loop(n_diag, n_blocks, trip, carry)
    values(n_blocks - 2)
    weights(carry)
    values(n_blocks - 1)
    o_ref[...] = jnp.where(lane < HEAD_DIM, acc_ref[0], acc_ref[1]).astype(o_ref.dtype)


def _sb_prompt(q, kb, vb, bias, *, tq, tk):
    b, t, _ = q.shape
    assert tq % tk == 0 and t % tq == 0
    kv_spec = pl.BlockSpec((None, LANES, t), lambda bi, p, i: (bi, p, 0))
    return pl.pallas_call(
        functools.partial(_sb_prompt_kernel, tq=tq, tk=tk),
        grid=(b, WIDTH // LANES, t // tq),
        in_specs=[pl.BlockSpec(memory_space=pltpu.SMEM),
                  pl.BlockSpec((None, tq, LANES), lambda bi, p, i: (bi, i, p)),
                  kv_spec, kv_spec],
        out_specs=pl.BlockSpec((None, tq, LANES), lambda bi, p, i: (bi, i, p)),
        out_shape=jax.ShapeDtypeStruct((b, t, WIDTH), BF16),
        scratch_shapes=[pltpu.VMEM((2, tq, LANES), F32), pltpu.VMEM((2, tq, tk), F32),
                        pltpu.VMEM((2, tq, tk), BF16), pltpu.VMEM((2, tq, tk), BF16)],
        compiler_params=pltpu.CompilerParams(dimension_semantics=("parallel", "parallel", "arbitrary"),
                                             vmem_limit_bytes=VMEM_LIMIT_BYTES),
        name="sb_prompt",
    )(bias, q, kb, vb)


def _sb_sample_kernel(pt_ref, q_ref, kn_ref, vn_ref, bias_ref, *rest, n_pages_step, t_new, page):
    kt_refs = rest[:n_pages_step]
    vt_refs = rest[n_pages_step:2 * n_pages_step]
    o_ref, qbd_ref, c_ref, acc_ref = rest[2 * n_pages_step:]
    del pt_ref
    j = pl.program_id(1)
    rows = N_HEADS * t_new
    r = lax.broadcasted_iota(jnp.int32, (page, page), 0)
    c = lax.broadcasted_iota(jnp.int32, (page, page), 1)
    tri = (r > c).astype(BF16)
    bias = bias_ref[...]

    def blocks(zs, avs):
        zs = [z + bias for z in zs]
        sps = [_softplus(z) for z in zs]
        sps_b = [sp.astype(BF16) for sp in sps]
        incl = [sp + _dot(sp_b, tri) for sp, sp_b in zip(sps, sps_b)]
        totals = [jnp.broadcast_to(t[:, 0:1], t.shape) for t in incl]
        run = c_ref[...]
        acc = acc_ref[...]
        for z, t, tot, av in zip(zs, incl, totals, avs):
            acc = acc + av(jnp.exp(z - (t + run)).astype(BF16))
            run = run + tot
        acc_ref[...] = acc
        c_ref[...] = run

    @pl.when(j == 0)
    def _():
        q = q_ref[...]
        qt = jnp.concatenate([q] * N_HEADS, axis=0)
        rr = lax.broadcasted_iota(jnp.int32, (rows, WIDTH), 0)
        cc = lax.broadcasted_iota(jnp.int32, (rows, WIDTH), 1)
        qbd_ref[...] = jnp.where(rr // t_new == cc // HEAD_DIM, qt, 0.0).astype(BF16)
        c_ref[...] = jnp.zeros_like(c_ref)
        acc_ref[...] = jnp.zeros_like(acc_ref)
        pad = jnp.zeros((page - t_new, WIDTH), F32)
        kn = jnp.concatenate([kn_ref[...], pad], axis=0).astype(BF16)
        vn = jnp.concatenate([vn_ref[...], pad], axis=0).astype(BF16)
        qrow = lax.broadcasted_iota(jnp.int32, (rows, page), 0) % t_new
        key = lax.broadcasted_iota(jnp.int32, (rows, page), 1)
        z = jnp.where(key < qrow, _dot_nt(qbd_ref[...], kn), -1e30)
        blocks([z], [lambda att: _dot(att, vn)])

    qbd = qbd_ref[...]
    blocks([_dot(qbd, kt_ref[...].astype(BF16)) for kt_ref in kt_refs],
           [lambda att, vt_ref=vt_ref: _dot_nt(att, vt_ref[...].astype(BF16)) for vt_ref in vt_refs])

    @pl.when(j == pl.num_programs(1) - 1)
    def _():
        lane_head = lax.broadcasted_iota(jnp.int32, (t_new, WIDTH), 1) // HEAD_DIM
        out = jnp.zeros((t_new, WIDTH), F32)
        for h in range(N_HEADS):
            out = out + jnp.where(lane_head == h, acc_ref[h * t_new:(h + 1) * t_new, :], 0.0)
        o_ref[...] = out.astype(o_ref.dtype)


def _sb_sample(q, k_new, v_new, cache_kt, cache_vt, page_table, bias_rows):
    s, t_new, _ = q.shape
    n_pages = page_table.shape[1]
    page = cache_kt.shape[2]
    g_pages = SAMPLE_PAGES_PER_STEP
    assert n_pages % g_pages == 0
    rows = N_HEADS * t_new

    def page_spec(g):
        return pl.BlockSpec((None, WIDTH, page),
                            lambda n, j, pt: (pt[n, n_pages - 1 - (j * g_pages + g)], 0, 0))

    seq_spec = pl.BlockSpec((None, t_new, WIDTH), lambda n, j, pt: (n, 0, 0))
    grid_spec = pltpu.PrefetchScalarGridSpec(
        num_scalar_prefetch=1,
        grid=(s, n_pages // g_pages),
        in_specs=[seq_spec, seq_spec, seq_spec,
                  pl.BlockSpec((rows, 1), lambda n, j, pt: (0, 0))]
                 + [page_spec(g) for g in range(g_pages)] * 2,
        out_specs=seq_spec,
        scratch_shapes=[pltpu.VMEM((rows, WIDTH), BF16),
                        pltpu.VMEM((rows, page), F32),
                        pltpu.VMEM((rows, WIDTH), F32)],
    )
    return pl.pallas_call(
        functools.partial(_sb_sample_kernel, n_pages_step=g_pages, t_new=t_new, page=page),
        grid_spec=grid_spec,
        out_shape=jax.ShapeDtypeStruct((s, t_new, WIDTH), BF16),
        compiler_params=pltpu.CompilerParams(dimension_semantics=("parallel", "arbitrary"),
                                             vmem_limit_bytes=VMEM_LIMIT_BYTES),
        name="sb_sample",
    )(page_table, q, k_new, v_new, bias_rows, *([cache_kt] * g_pages), *([cache_vt] * g_pages))


def _rwkv_kernel(rw_ref, prev0_ref, s0_ref, mu_ref, w0_ref, a0_ref, kk_ref, ka_ref, rk_ref, lnw_ref, lnb_ref,
                 wd_ref, wa_ref, wg_ref, e_ref,
                 o_ref, sout_ref,
                 sbd_scr, prev_scr, *, nb, tb, chunk):
    assert chunk <= HEAD_DIM and chunk % RWKV_SUBBLOCK == 0
    j = pl.program_id(1)
    n_pairs = WIDTH // LANES
    low = lax.broadcasted_iota(jnp.int32, (1, LANES), 1) < HEAD_DIM

    @pl.when(j == 0)
    def _():
        prev_scr[...] = prev0_ref[...]
        zero = jnp.zeros((HEAD_DIM, HEAD_DIM), F32)
        for b in range(nb):
            for p in range(n_pairs):
                top = jnp.concatenate([s0_ref[b, 2 * p], zero], axis=1)
                bot = jnp.concatenate([zero, s0_ref[b, 2 * p + 1]], axis=1)
                sbd_scr[b, p] = jnp.concatenate([top, bot], axis=0)

    e_mat = e_ref[...]
    row = lax.broadcasted_iota(jnp.int32, (chunk, 1), 0)
    bf = lambda x: x.astype(BF16)

    def head_sum(x):
        hi, lo = _split(x)
        return _dot(hi, e_mat) + _dot(lo, e_mat)

    def prepare(b):
        rw = rw_ref[b]
        if tb < chunk:
            rw = jnp.concatenate([rw, jnp.zeros((chunk - tb, RW_PAD), F32)], axis=0)
        prev = jnp.where(row == 0, prev_scr[b], pltpu.roll(rw, 1, axis=0))
        xs = rw + (prev - rw) * mu_ref[...]
        prev_scr[b] = rw[tb - 1:tb, :]

        r = xs[:, RW_R:RW_R + WIDTH]
        k = xs[:, RW_K:RW_K + WIDTH]
        v = xs[:, RW_V:RW_V + WIDTH]
        wd = xs[:, RW_WD:RW_WD + WD_SLOT]
        ad = xs[:, RW_AD:RW_AD + AD_SLOT]
        gd = xs[:, RW_GD:RW_GD + GD_SLOT]

        x_w = w0_ref[...] + _dot(bf(jnp.tanh(wd)), wd_ref[...])
        w_log = -_softplus(-x_w) - 0.5
        lw = -jnp.exp(w_log)
        a_sig = _sigmoid(a0_ref[...] + _dot(bf(ad), wa_ref[...]))
        g = _dot(bf(_sigmoid(gd)), wg_ref[...])
        kkr = k * kk_ref[...]
        kk = kkr / jnp.maximum(jnp.sqrt(head_sum(kkr * kkr)), 1e-12)
        kmod = k * (1.0 + (a_sig - 1.0) * ka_ref[...])
        if tb < chunk:
            valid = row < tb
            lw = jnp.where(valid, lw, 0.0)
            kk = jnp.where(valid, kk, 0.0)
            kmod = jnp.where(valid, kmod, 0.0)

        lc = lw
        shift = 1
        while shift < chunk:
            lc = lc + jnp.where(row >= shift, pltpu.roll(lc, shift, axis=0), 0.0)
            shift *= 2
        lprev = lc - lw
        lref = lc[chunk // 2 - 1:chunk // 2, :]
        lend = lc[chunk - 1:chunk, :]
        e_b = jnp.exp(lref - lc)
        e_e = jnp.exp(lend - lc)
        b_vec = kk * a_sig
        return dict(
            r=r, v=v, kmod=kmod, g=g,
            a_hat=-kk * jnp.exp(lprev - lref), a_start=-kk * jnp.exp(lprev),
            r_hat=r * jnp.exp(lc - lref), r_start=r * jnp.exp(lc),
            b_hat=b_vec * e_b, k_hat=kmod * e_b, b_til=b_vec * e_e, k_til=kmod * e_e,
            dec_end=jnp.exp(lend))

    ss = lax.broadcasted_iota(jnp.int32, (chunk, LANES), 0)
    tt = lax.broadcasted_iota(jnp.int32, (chunk, LANES), 1) % HEAD_DIM
    same_head = (lax.broadcasted_iota(jnp.int32, (LANES, LANES), 0) // HEAD_DIM
                 == lax.broadcasted_iota(jnp.int32, (LANES, LANES), 1) // HEAD_DIM)

    def stack2(x):
        parts = [jnp.where(low, x, 0.0), jnp.where(low, 0.0, x)]
        if chunk < HEAD_DIM:
            pad = jnp.zeros((HEAD_DIM - chunk, LANES), x.dtype)
            parts = [parts[0], pad, parts[1], pad]
        return jnp.concatenate(parts, axis=0)

    seqs = [prepare(b) for b in range(nb)]
    chains = [(b, p) for b in range(nb) for p in range(n_pairs)]
    part = lambda b, p, name: seqs[b][name][:, p * LANES:(p + 1) * LANES]

    sub = RWKV_SUBBLOCK
    n_sub = -(-min(chunk, tb) // sub)
    same_sub = ss // sub == tt // sub
    each = lambda f: [f(i, b, p) for i, (b, p) in enumerate(chains)]
    sbds = each(lambda i, b, p: sbd_scr[b, p])
    ahs = each(lambda i, b, p: bf(part(b, p, "a_hat")))
    bh2 = each(lambda i, b, p: bf(stack2(part(b, p, "b_hat"))))
    kh2 = each(lambda i, b, p: bf(stack2(part(b, p, "k_hat"))))
    v2 = each(lambda i, b, p: bf(stack2(part(b, p, "v"))))
    mts = each(lambda i, b, p: _dot_nt(bf(part(b, p, "b_hat")), bf(stack2(part(b, p, "a_hat")))))
    ms = each(lambda i, b, p: jnp.where(tt < ss, _dot_nt(ahs[i], bh2[i]), 0.0))
    m_aks = each(lambda i, b, p: jnp.where(tt < ss, _dot_nt(ahs[i], kh2[i]), 0.0))
    rhs0 = each(lambda i, b, p: _dot_nt(bf(part(b, p, "a_start")), bf(sbds[i])))
    rhss = each(lambda i, b, p: rhs0[i] + _dot(bf(m_aks[i]), v2[i]))
    def pack(x):
        d = jnp.where(same_sub, x, 0.0)
        out = d[0:sub]
        for blk in range(1, n_sub):
            out = out + d[blk * sub:(blk + 1) * sub]
        return out

    mtps = [pack(jnp.where(ss < tt, mt, 0.0)) for mt in mts]
    eps = [pack(m) for m in ms]
    prow = lax.broadcasted_iota(jnp.int32, (sub, 1), 0)
    group0 = lax.broadcasted_iota(jnp.int32, (sub, LANES), 1) // sub * sub
    for tau in range(1, min(sub, tb)):
        for i in range(len(chains)):
            coef = jnp.take_along_axis(mtps[i], group0 + tau, axis=1)
            contrib = jnp.sum(coef * eps[i], axis=0, keepdims=True)
            eps[i] = jnp.where(prow == tau, eps[i] + contrib, eps[i])

    def unpack(x):
        blk_of_lane = lax.broadcasted_iota(jnp.int32, (sub, LANES), 1) // sub % (HEAD_DIM // sub)
        rows = [jnp.where(blk_of_lane == blk, x, 0.0) if blk < n_sub else jnp.zeros((sub, LANES), F32)
                for blk in range(chunk // sub)]
        return jnp.concatenate(rows, axis=0)

    es = [unpack(x) for x in eps]
    size = sub
    while size < min(chunk, tb):
        coupled = ((ss // size) % 2 == 1) & (tt // size == ss // size - 1)
        ns = [jnp.where(coupled, m, 0.0) for m in ms]
        qs = each(lambda i, b, p: ns[i] + _dot3(es[i], stack2(ns[i])))
        es = each(lambda i, b, p: es[i] + qs[i] + _dot3(qs[i], stack2(es[i])))
        size *= 2
    us = each(lambda i, b, p: rhss[i] + _dot3(es[i], stack2(rhss[i])))

    rhs_b = each(lambda i, b, p: bf(part(b, p, "r_hat")))
    m_rbs = each(lambda i, b, p: jnp.where(tt <= ss, _dot_nt(rhs_b[i], bh2[i]), 0.0))
    m_rks = each(lambda i, b, p: jnp.where(tt <= ss, _dot_nt(rhs_b[i], kh2[i]), 0.0))
    y0 = each(lambda i, b, p: _dot_nt(bf(part(b, p, "r_start")), bf(sbds[i])))
    ys = each(lambda i, b, p: y0[i] + _dot(bf(jnp.concatenate([m_rbs[i], m_rks[i]], axis=1)),
                                          jnp.concatenate([bf(stack2(us[i])), v2[i]], axis=0)))
    upds = each(lambda i, b, p: _dot_tn(bf(jnp.concatenate([us[i], part(b, p, "v")], axis=0)),
                                        bf(jnp.concatenate([part(b, p, "b_til"), part(b, p, "k_til")], axis=0))))
    y_parts = [[] for _ in range(nb)]
    for i, (b, p) in enumerate(chains):
        y_parts[b].append(ys[i])
        sbd_scr[b, p] = jnp.where(same_head, sbds[i] * part(b, p, "dec_end") + upds[i], 0.0)

    inv_n = 1.0 / HEAD_DIM
    for b in range(nb):
        s = seqs[b]
        y = jnp.concatenate(y_parts[b], axis=-1)
        mean = head_sum(y) * inv_n
        d = y - mean
        var = head_sum(d * d) * inv_n
        yn = d * lax.rsqrt(var + GN_EPS) * lnw_ref[...] + lnb_ref[...]
        bonus = head_sum(s["r"] * s["kmod"] * rk_ref[...]) * s["v"]
        out = (yn + bonus) * s["g"]
        o_ref[b] = out[:tb].astype(o_ref.dtype)

    @pl.when(j == pl.num_programs(1) - 1)
    def _():
        for b in range(nb):
            for p in range(n_pairs):
                sbd = sbd_scr[b, p]
                sout_ref[b, 2 * p] = sbd[:HEAD_DIM, :HEAD_DIM]
                sout_ref[b, 2 * p + 1] = sbd[HEAD_DIM:, HEAD_DIM:]


def _rwkv(rw, prev0, s0, mu, w0, a0, k_k, k_a, r_k, lnx_w, lnx_b, wd_up, wa_up, wg_up, e_mat, *, tb, nb):
    b, t, _ = rw.shape
    chunk = min(RWKV_CHUNK, -(-tb // RWKV_SUBBLOCK) * RWKV_SUBBLOCK)
    assert t % tb == 0 and tb <= chunk and b % nb == 0
    vec = _const_spec((1, WIDTH))
    state_spec = pl.BlockSpec((nb, N_HEADS, HEAD_DIM, HEAD_DIM), lambda bi, j: (bi, 0, 0, 0))
    return pl.pallas_call(
        functools.partial(_rwkv_kernel, nb=nb, tb=tb, chunk=chunk),
        grid=(b // nb, t // tb),
        in_specs=[pl.BlockSpec((nb, tb, RW_PAD), lambda bi, j: (bi, j, 0)),
                  pl.BlockSpec((nb, 1, RW_PAD), lambda bi, j: (bi, 0, 0)),
                  state_spec,
                  _const_spec((1, RW_PAD)),
                  vec, vec, vec, vec, vec, vec, vec,
                  _const_spec((WD_SLOT, WIDTH)), _const_spec((AD_SLOT, WIDTH)), _const_spec((GD_SLOT, WIDTH)),
                  _const_spec((WIDTH, WIDTH))],
        out_specs=[pl.BlockSpec((nb, tb, WIDTH), lambda bi, j: (bi, j, 0)), state_spec],
        out_shape=[jax.ShapeDtypeStruct((b, t, WIDTH), BF16),
                   jax.ShapeDtypeStruct((b, N_HEADS, HEAD_DIM, HEAD_DIM), F32)],
        scratch_shapes=[pltpu.VMEM((nb, WIDTH // LANES, LANES, LANES), F32),
                        pltpu.VMEM((nb, 1, RW_PAD), F32)],
        compiler_params=pltpu.CompilerParams(dimension_semantics=("parallel", "arbitrary"),
                                             vmem_limit_bytes=VMEM_LIMIT_BYTES),
        name="rwkv",
    )(rw, prev0, s0, mu, w0, a0, k_k, k_a, r_k, lnx_w, lnx_b, wd_up, wa_up, wg_up, e_mat)


def _merge_ffn_kernel(x_ref, osb_ref, orw_ref, gate_ref, wsb_ref, wrw_ref, wo_ref, gffn_ref, win_ref, wout_ref,
                      gfin_ref, y_ref, *, ff_chunk):
    gates = gate_ref[...]
    merged = (gates[:, :D_MODEL] * _dot(osb_ref[...], wsb_ref[...])
              + gates[:, D_MODEL:] * _dot(orw_ref[...], wrw_ref[...]))
    x = x_ref[...] + _dot(merged.astype(BF16), wo_ref[...])
    ms = jnp.mean(x * x, axis=-1, keepdims=True)
    h = (x * lax.rsqrt(ms + RMS_EPS) * gffn_ref[...]).astype(BF16)
    for c in range(D_FF // ff_chunk):
        lo = c * ff_chunk
        gate = _dot(h, win_ref[:, lo:lo + ff_chunk])
        up = _dot(h, win_ref[:, D_FF + lo:D_FF + lo + ff_chunk])
        act = (gate * _sigmoid(gate) * up).astype(BF16)
        x = x + _dot(act, wout_ref[lo:lo + ff_chunk, :])
    ms = jnp.mean(x * x, axis=-1, keepdims=True)
    y_ref[...] = x * lax.rsqrt(ms + RMS_EPS) * gfin_ref[...]


def _merge_ffn(x, o_sb, o_rw, gates, w_up_sb, w_up_rw, w_o, g_ffn, w_ffn_in, w_ffn_out, g_final, *, tm):
    n = x.shape[0]
    row = lambda w: pl.BlockSpec((tm, w), lambda i: (i, 0))
    return pl.pallas_call(
        functools.partial(_merge_ffn_kernel, ff_chunk=D_FF // 2),
        grid=(n // tm,),
        in_specs=[row(D_MODEL), row(WIDTH), row(WIDTH), row(2 * D_MODEL),
                  _const_spec((WIDTH, D_MODEL)), _const_spec((WIDTH, D_MODEL)), _const_spec((D_MODEL, D_MODEL)),
                  _const_spec((1, D_MODEL)), _const_spec((D_MODEL, 2 * D_FF)), _const_spec((D_FF, D_MODEL)),
                  _const_spec((1, D_MODEL))],
        out_specs=row(D_MODEL),
        out_shape=jax.ShapeDtypeStruct((n, D_MODEL), F32),
        compiler_params=pltpu.CompilerParams(dimension_semantics=("parallel",),
                                             vmem_limit_bytes=VMEM_LIMIT_BYTES),
        name="merge_ffn",
    )(x, o_sb, o_rw, gates, w_up_sb, w_up_rw, w_o, g_ffn, w_ffn_in, w_ffn_out, g_final)


def kernel(x_prompt, x_sample, cache_k, cache_v, page_table, state_wkv, state_shift, norm_mix_g, w_in, sb_bias, b_gate, mu_shift, w0, w_decay_up, a0, w_a_up, w_g_up, k_k, k_a, r_k, lnx_w, lnx_b, w_up_sb, w_up_rw, w_o, norm_ffn_g, w_ffn_in, w_ffn_out, norm_final_g):
    depth = w_in.shape[0]
    assert depth == 1
    bp, tp, _ = x_prompt.shape
    bs, ts, _ = x_sample.shape
    l = 0

    w = w_in[l]
    att_cols = 3 * WIDTH
    wq = w[:, 0:WIDTH].astype(BF16)
    wk = w[:, WIDTH:2 * WIDTH].astype(BF16)
    wv = w[:, 2 * WIDTH:att_cols].astype(BF16)
    wrw = _pad_rw(w[:, att_cols:att_cols + RW_COLS]).astype(BF16)
    wg = w[:, att_cols + RW_COLS:].astype(BF16)
    bg = b_gate[l][None, :]
    g_mix = norm_mix_g[l][None, :]
    mu = _pad_rw(mu_shift[l])[None, :]
    pad_rows = lambda m, n: jnp.pad(m, ((0, n - m.shape[0]), (0, 0))).astype(BF16)
    wd_up = pad_rows(w_decay_up[l], WD_SLOT)
    wa_up = pad_rows(w_a_up[l], AD_SLOT)
    wg_up = pad_rows(w_g_up[l], GD_SLOT)
    vec = lambda p: p.reshape(1, WIDTH)
    rw_vecs = (vec(w0[l]), vec(a0[l]), vec(k_k[l]), vec(k_a[l]), vec(r_k[l]), vec(lnx_w[l]), vec(lnx_b[l]))
    ch = jnp.arange(WIDTH, dtype=jnp.int32) // HEAD_DIM
    e_mat = (ch[:, None] == ch[None, :]).astype(BF16)
    ffn_w = (w_up_sb[l].astype(BF16), w_up_rw[l].astype(BF16), w_o[l].astype(BF16), norm_ffn_g[l][None, :],
             w_ffn_in[l].astype(BF16), w_ffn_out[l].astype(BF16), norm_final_g[None, :])
    bias = sb_bias[l]

    q, kt, vt, ktb, vtb, rw, gates = _inproj(x_prompt.reshape(bp * tp, D_MODEL), g_mix, wq, wk.T, wv.T, wrw, wg, bg,
                                            tm=512, seq_len=tp)
    o_sb = _sb_prompt(q.reshape(bp, tp, WIDTH), ktb, vtb, bias, tq=SB_QUERY_BLOCK, tk=SB_KEY_BLOCK)
    rw3 = rw.reshape(bp, tp, RW_PAD)
    o_rw, wkv_p = _rwkv(rw3, jnp.zeros((bp, 1, RW_PAD), F32), jnp.zeros((bp, N_HEADS, HEAD_DIM, HEAD_DIM), F32),
                        mu, *rw_vecs, wd_up, wa_up, wg_up, e_mat, tb=RWKV_CHUNK, nb=RWKV_PROMPT_SEQS)
    y_p = _merge_ffn(x_prompt.reshape(bp * tp, D_MODEL), o_sb.reshape(bp * tp, WIDTH), o_rw.reshape(bp * tp, WIDTH),
                     gates, *ffn_w, tm=512)
    token_major = lambda a: jnp.transpose(a.reshape(bp, N_HEADS, HEAD_DIM, tp), (0, 3, 1, 2))[None]
    k_prompt = token_major(kt)
    v_prompt = token_major(vt)
    shift_p = _unpad_rw(rw3[:, -1, :])[None]

    q, k, v, kb, vb, rw, gates = _inproj(x_sample.reshape(bs * ts, D_MODEL), g_mix, wq, wk, wv, wrw, wg, bg, tm=512)
    sh = (bs, ts, WIDTH)
    n_phys, page = cache_k.shape[1], cache_k.shape[2]
    bias_rows = jnp.repeat(bias, ts)[:, None]
    pages_t = lambda c: jnp.transpose(c, (0, 1, 3, 4, 2)).reshape(depth * n_phys, WIDTH, page)
    o_sb = _sb_sample(q.reshape(sh), k.reshape(sh), v.reshape(sh), pages_t(cache_k), pages_t(cache_v),
                      page_table + l * n_phys, bias_rows)
    rw3 = rw.reshape(bs, ts, RW_PAD)
    o_rw, wkv_s = _rwkv(rw3, _pad_rw(state_shift[l])[:, None, :], state_wkv[l],
                        mu, *rw_vecs, wd_up, wa_up, wg_up, e_mat, tb=ts, nb=RWKV_SAMPLE_SEQS)
    y_s = _merge_ffn(x_sample.reshape(bs * ts, D_MODEL), o_sb.reshape(bs * ts, WIDTH), o_rw.reshape(bs * ts, WIDTH),
                     gates, *ffn_w, tm=512)
    k_sample = k.reshape(1, bs, ts, N_HEADS, HEAD_DIM)
    v_sample = v.reshape(1, bs, ts, N_HEADS, HEAD_DIM)
    shift_s = _unpad_rw(rw3[:, -1, :])[None]

    return (y_p.reshape(bp, tp, D_MODEL), y_s.reshape(bs, ts, D_MODEL),
            k_prompt, v_prompt, wkv_p[None], shift_p,
            k_sample, v_sample, wkv_s[None], shift_s)
```

```python
import functools

import jax
import jax.numpy as jnp
from jax import lax
from jax.experimental import pallas as pl
from jax.experimental.pallas import tpu as pltpu

F32 = jnp.float32
BF16 = jnp.bfloat16

D_MODEL = 1024
HEAD_DIM = 64
N_HEADS = 8
WIDTH = N_HEADS * HEAD_DIM
DECAY_LORA = 64
AAA_LORA = 64
GATE_LORA = 160
RW_COLS = 3 * WIDTH + DECAY_LORA + AAA_LORA + GATE_LORA
D_FF = 2816
RMS_EPS = 1e-6
GN_EPS = 64e-5
LOG2_E = 1.4426950408889634

LANES = 128
RW_R, RW_K, RW_V = 0, WIDTH, 2 * WIDTH
RW_WD, RW_AD, RW_GD = 3 * WIDTH, 3 * WIDTH + LANES, 3 * WIDTH + 2 * LANES
RW_PAD = 3 * WIDTH + 4 * LANES
WD_SLOT, AD_SLOT, GD_SLOT = LANES, LANES, 2 * LANES

VMEM_LIMIT_BYTES = 56 * 1024 * 1024

RWKV_CHUNK = 64
RWKV_SUBBLOCK = 16
RWKV_PROMPT_SEQS = 4
RWKV_SAMPLE_SEQS = 8
SB_QUERY_BLOCK = 512
SB_KEY_BLOCK = 256
SB_PAIRS_PER_STEP = 1
SAMPLE_PAGES_PER_STEP = 32


def _dot(a, b):
    return jnp.dot(a, b, preferred_element_type=F32)


def _dot_nt(a, b):
    return lax.dot_general(a, b, (((1,), (1,)), ((), ())), preferred_element_type=F32)


def _dot_tn(a, b):
    return lax.dot_general(a, b, (((0,), (0,)), ((), ())), preferred_element_type=F32)


def _split(x):
    hi = x.astype(BF16)
    lo = (x - hi.astype(F32)).astype(BF16)
    return hi, lo


def _dot3(a, b):
    a_hi, a_lo = _split(a)
    b_hi, b_lo = _split(b)
    n = a.shape[0]
    both = _dot(jnp.concatenate([a_hi, a_lo], axis=0), b_hi)
    return both[:n] + both[n:] + _dot(a_hi, b_lo)


def _softplus(z):
    return jnp.maximum(z, 0.0) + jnp.log(1.0 + jnp.exp2(jnp.abs(z) * (-LOG2_E)))


def _sigmoid(x):
    return 1.0 / (1.0 + jnp.exp(-x))


def _const_spec(shape):
    nd = len(shape)
    return pl.BlockSpec(shape, lambda *_: (0,) * nd, pipeline_mode=pl.Buffered(1))


def _pad_rw(x):
    def slot(lo, width, size):
        part = x[..., lo:lo + width]
        pad = [(0, 0)] * (x.ndim - 1) + [(0, size - width)]
        return jnp.pad(part, pad)
    o = 3 * WIDTH
    return jnp.concatenate([
        x[..., :o],
        slot(o, DECAY_LORA, WD_SLOT),
        slot(o + DECAY_LORA, AAA_LORA, AD_SLOT),
        slot(o + DECAY_LORA + AAA_LORA, GATE_LORA, GD_SLOT),
    ], axis=-1)


def _unpad_rw(x):
    return jnp.concatenate([
        x[..., :3 * WIDTH],
        x[..., RW_WD:RW_WD + DECAY_LORA],
        x[..., RW_AD:RW_AD + AAA_LORA],
        x[..., RW_GD:RW_GD + GATE_LORA],
    ], axis=-1)


def _inproj_kernel(x_ref, g_ref, wq_ref, wk_ref, wv_ref, wrw_ref, wg_ref, bg_ref,
                   q_ref, k_ref, v_ref, kb_ref, vb_ref, rw_ref, gate_ref, *, channel_major_kv):
    x = x_ref[...]
    ms = jnp.mean(x * x, axis=-1, keepdims=True)
    h = (x * lax.rsqrt(ms + RMS_EPS) * g_ref[...]).astype(BF16)
    q_ref[...] = _dot(h, wq_ref[...]) * (HEAD_DIM ** -0.5)
    proj = (lambda w_ref: _dot_nt(w_ref[...], h)) if channel_major_kv else (lambda w_ref: _dot(h, w_ref[...]))
    k = proj(wk_ref)
    k_ref[...] = k
    kb_ref[...] = k.astype(BF16)
    v = proj(wv_ref)
    v_ref[...] = v
    vb_ref[...] = v.astype(BF16)
    rw_ref[...] = _dot(h, wrw_ref[...])
    gate_ref[...] = _sigmoid(_dot(h, wg_ref[...]) + bg_ref[...])


def _inproj(x, g, wq, wk, wv, wrw, wg, bg, *, tm, seq_len=None):
    n = x.shape[0]
    row = lambda w: pl.BlockSpec((tm, w), lambda i: (i, 0))
    if seq_len is None:
        kv_spec, kv_shape, w_kv = row(WIDTH), (n, WIDTH), _const_spec((D_MODEL, WIDTH))
    else:
        assert seq_len % tm == 0
        tiles = seq_len // tm
        kv_spec = pl.BlockSpec((None, WIDTH, tm), lambda i: (i // tiles, 0, i % tiles))
        kv_shape, w_kv = (n // seq_len, WIDTH, seq_len), _const_spec((WIDTH, D_MODEL))
    return pl.pallas_call(
        functools.partial(_inproj_kernel, channel_major_kv=seq_len is not None),
        grid=(n // tm,),
        in_specs=[row(D_MODEL), _const_spec((1, D_MODEL)),
                  _const_spec((D_MODEL, WIDTH)), w_kv, w_kv,
                  _const_spec((D_MODEL, RW_PAD)), _const_spec((D_MODEL, 2 * D_MODEL)),
                  _const_spec((1, 2 * D_MODEL))],
        out_specs=[row(WIDTH), kv_spec, kv_spec, kv_spec, kv_spec, row(RW_PAD), row(2 * D_MODEL)],
        out_shape=[jax.ShapeDtypeStruct((n, WIDTH), F32),
                   jax.ShapeDtypeStruct(kv_shape, F32),
                   jax.ShapeDtypeStruct(kv_shape, F32),
                   jax.ShapeDtypeStruct(kv_shape, BF16),
                   jax.ShapeDtypeStruct(kv_shape, BF16),
                   jax.ShapeDtypeStruct((n, RW_PAD), F32),
                   jax.ShapeDtypeStruct((n, 2 * D_MODEL), F32)],
        compiler_params=pltpu.CompilerParams(dimension_semantics=("parallel",),
                                             vmem_limit_bytes=VMEM_LIMIT_BYTES),
        name="inproj",
    )(x, g, wq, wk, wv, wrw, wg, bg)


def _sb_prompt_kernel(bias_ref, q_ref, k_ref, v_ref, o_ref, acc_ref, d_scr, sp_scr, att_scr, *, tq, tk, pairs):
    group = pl.program_id(1)
    qi = pl.program_id(2)
    n_heads = 2 * pairs
    lane = lax.broadcasted_iota(jnp.int32, (1, LANES), 1)
    qh, bias = [], []
    for g in range(pairs):
        q = q_ref[:, g * LANES:(g + 1) * LANES]
        qh += [jnp.where(lane < HEAD_DIM, q, 0.0).astype(BF16), jnp.where(lane >= HEAD_DIM, q, 0.0).astype(BF16)]
        bias += [bias_ref[2 * (pairs * group + g)], bias_ref[2 * (pairs * group + g) + 1]]
    tri_r = lax.broadcasted_iota(jnp.int32, (tk, tk), 0)
    tri_c = lax.broadcasted_iota(jnp.int32, (tk, tk), 1)
    tri = (tri_r > tri_c).astype(BF16)
    r = lax.broadcasted_iota(jnp.int32, (tq, tk), 0)
    c = lax.broadcasted_iota(jnp.int32, (tq, tk), 1)
    acc_ref[...] = jnp.zeros_like(acc_ref)
    n_diag = tq // tk
    n_blocks = n_diag * (qi + 1)

    def key_block(p):
        start = pl.multiple_of(jnp.clip(n_blocks - 1 - p, 0, n_blocks - 1) * tk, tk)
        return pl.ds(start, tk)

    def scores(p, visible):
        keys = key_block(p)
        for h in range(n_heads):
            g = h // 2
            z = _dot(qh[h], k_ref[g * LANES:(g + 1) * LANES, keys]) + bias[h]
            if visible is not None:
                z = jnp.where(visible, z, -1e30)
            sp = _softplus(z)
            d_scr[h] = z - sp
            sp_scr[h] = sp.astype(BF16)

    def weights(carry):
        out = []
        for h in range(n_heads):
            sp = sp_scr[h]
            tail = _dot(sp, tri) + carry[h]
            att_scr[h] = jnp.exp(d_scr[h] - tail).astype(BF16)
            out.append(tail[:, 0:1] + sp[:, 0:1].astype(F32))
        return tuple(out)

    def values(p):
        keys = key_block(p)
        for h in range(n_heads):
            g = h // 2
            acc_ref[h] += _dot_nt(att_scr[h], v_ref[g * LANES:(g + 1) * LANES, keys])

    def trip(it, carry, visible=None):
        values(it - 2)
        carry = weights(carry)
        scores(it, visible)
        return carry

    def visible(p):
        return c + (n_diag - 1 - p) * tk < r

    scores(0, visible(0))
    carry = (jnp.zeros((tq, 1), F32),) * n_heads
    if n_diag == 1:
        att_scr[...] = jnp.zeros_like(att_scr)
    else:
        carry = weights(carry)
        scores(1, visible(1))
        for it in range(2, n_diag):
            carry = trip(it, carry, visible(it))
    def trips(i, carry):
        for k in range(n_diag):
            carry = trip(n_diag * (i + 1) + k, carry)
        return carry

    carry = lax.fori_loop(0, qi, trips, carry)
    values(n_blocks - 2)
    weights(carry)
    values(n_blocks - 1)
    for g in range(pairs):
        o_ref[:, g * LANES:(g + 1) * LANES] = jnp.where(
            lane < HEAD_DIM, acc_ref[2 * g], acc_ref[2 * g + 1]).astype(o_ref.dtype)


def _sb_prompt(q, kb, vb, bias, *, tq, tk, pairs):
    b, t, _ = q.shape
    width = pairs * LANES
    assert tq % tk == 0 and t % tq == 0 and WIDTH % width == 0
    kv_spec = pl.BlockSpec((None, width, t), lambda bi, p, i: (bi, p, 0))
    return pl.pallas_call(
        functools.partial(_sb_prompt_kernel, tq=tq, tk=tk, pairs=pairs),
        grid=(b, WIDTH // width, t // tq),
        in_specs=[pl.BlockSpec(memory_space=pltpu.SMEM),
                  pl.BlockSpec((None, tq, width), lambda bi, p, i: (bi, i, p)),
                  kv_spec, kv_spec],
        out_specs=pl.BlockSpec((None, tq, width), lambda bi, p, i: (bi, i, p)),
        out_shape=jax.ShapeDtypeStruct((b, t, WIDTH), BF16),
        scratch_shapes=[pltpu.VMEM((2 * pairs, tq, LANES), F32), pltpu.VMEM((2 * pairs, tq, tk), F32),
                        pltpu.VMEM((2 * pairs, tq, tk), BF16), pltpu.VMEM((2 * pairs, tq, tk), BF16)],
        compiler_params=pltpu.CompilerParams(dimension_semantics=("parallel", "parallel", "arbitrary"),
                                             vmem_limit_bytes=VMEM_LIMIT_BYTES),
        name="sb_prompt",
    )(bias, q, kb, vb)


def _sb_sample_kernel(pt_ref, q_ref, kn_ref, vn_ref, bias_ref, *rest, n_pages_step, t_new, page):
    kt_refs = rest[:n_pages_step]
    vt_refs = rest[n_pages_step:2 * n_pages_step]
    o_ref, qbd_ref, c_ref, acc_ref = rest[2 * n_pages_step:]
    del pt_ref
    j = pl.program_id(1)
    rows = N_HEADS * t_new
    r = lax.broadcasted_iota(jnp.int32, (page, page), 0)
    c = lax.broadcasted_iota(jnp.int32, (page, page), 1)
    tri = (r > c).astype(BF16)
    bias = bias_ref[...]

    def blocks(zs, avs):
        zs = [z + bias for z in zs]
        sps = [_softplus(z) for z in zs]
        sps_b = [sp.astype(BF16) for sp in sps]
        incl = [sp + _dot(sp_b, tri) for sp, sp_b in zip(sps, sps_b)]
        totals = [jnp.broadcast_to(t[:, 0:1], t.shape) for t in incl]
        run = c_ref[...]
        acc = acc_ref[...]
        for z, t, tot, av in zip(zs, incl, totals, avs):
            acc = acc + av(jnp.exp(z - (t + run)).astype(BF16))
            run = run + tot
        acc_ref[...] = acc
        c_ref[...] = run

    @pl.when(j == 0)
    def _():
        q = q_ref[...]
        qt = jnp.concatenate([q] * N_HEADS, axis=0)
        rr = lax.broadcasted_iota(jnp.int32, (rows, WIDTH), 0)
        cc = lax.broadcasted_iota(jnp.int32, (rows, WIDTH), 1)
        qbd_ref[...] = jnp.where(rr // t_new == cc // HEAD_DIM, qt, 0.0).astype(BF16)
        c_ref[...] = jnp.zeros_like(c_ref)
        acc_ref[...] = jnp.zeros_like(acc_ref)
        pad = jnp.zeros((page - t_new, WIDTH), F32)
        kn = jnp.concatenate([kn_ref[...], pad], axis=0).astype(BF16)
        vn = jnp.concatenate([vn_ref[...], pad], axis=0).astype(BF16)
        qrow = lax.broadcasted_iota(jnp.int32, (rows, page), 0) % t_new
        key = lax.broadcasted_iota(jnp.int32, (rows, page), 1)
        z = jnp.where(key < qrow, _dot_nt(qbd_ref[...], kn), -1e30)
        blocks([z], [lambda att: _dot(att, vn)])

    qbd = qbd_ref[...]
    blocks([_dot(qbd, kt_ref[...].astype(BF16)) for kt_ref in kt_refs],
           [lambda att, vt_ref=vt_ref: _dot_nt(att, vt_ref[...].astype(BF16)) for vt_ref in vt_refs])

    @pl.when(j == pl.num_programs(1) - 1)
    def _():
        lane_head = lax.broadcasted_iota(jnp.int32, (t_new, WIDTH), 1) // HEAD_DIM
        out = jnp.zeros((t_new, WIDTH), F32)
        for h in range(N_HEADS):
            out = out + jnp.where(lane_head == h, acc_ref[h * t_new:(h + 1) * t_new, :], 0.0)
        o_ref[...] = out.astype(o_ref.dtype)


def _sb_sample(q, k_new, v_new, cache_kt, cache_vt, page_table, bias_rows):
    s, t_new, _ = q.shape
    n_pages = page_table.shape[1]
    page = cache_kt.shape[2]
    g_pages = SAMPLE_PAGES_PER_STEP
    assert n_pages % g_pages == 0
    rows = N_HEADS * t_new

    def page_spec(g):
        return pl.BlockSpec((None, WIDTH, page),
                            lambda n, j, pt: (pt[n, n_pages - 1 - (j * g_pages + g)], 0, 0))

    seq_spec = pl.BlockSpec((None, t_new, WIDTH), lambda n, j, pt: (n, 0, 0))
    grid_spec = pltpu.PrefetchScalarGridSpec(
        num_scalar_prefetch=1,
        grid=(s, n_pages // g_pages),
        in_specs=[seq_spec, seq_spec, seq_spec,
                  pl.BlockSpec((rows, 1), lambda n, j, pt: (0, 0))]
                 + [page_spec(g) for g in range(g_pages)] * 2,
        out_specs=seq_spec,
        scratch_shapes=[pltpu.VMEM((rows, WIDTH), BF16),
                        pltpu.VMEM((rows, page), F32),
                        pltpu.VMEM((rows, WIDTH), F32)],
    )
    return pl.pallas_call(
        functools.partial(_sb_sample_kernel, n_pages_step=g_pages, t_new=t_new, page=page),
        grid_spec=grid_spec,
        out_shape=jax.ShapeDtypeStruct((s, t_new, WIDTH), BF16),
        compiler_params=pltpu.CompilerParams(dimension_semantics=("parallel", "arbitrary"),
                                             vmem_limit_bytes=VMEM_LIMIT_BYTES),
        name="sb_sample",
    )(page_table, q, k_new, v_new, bias_rows, *([cache_kt] * g_pages), *([cache_vt] * g_pages))


def _rwkv_kernel(rw_ref, prev0_ref, s0_ref, mu_ref, w0_ref, a0_ref, kk_ref, ka_ref, rk_ref, lnw_ref, lnb_ref,
                 wd_ref, wa_ref, wg_ref, e_ref,
                 o_ref, sout_ref,
                 sbd_scr, prev_scr, *, nb, tb, chunk):
    assert chunk <= HEAD_DIM and chunk % RWKV_SUBBLOCK == 0
    j = pl.program_id(1)
    n_pairs = WIDTH // LANES
    low = lax.broadcasted_iota(jnp.int32, (1, LANES), 1) < HEAD_DIM

    @pl.when(j == 0)
    def _():
        prev_scr[...] = prev0_ref[...]
        zero = jnp.zeros((HEAD_DIM, HEAD_DIM), F32)
        for b in range(nb):
            for p in range(n_pairs):
                top = jnp.concatenate([s0_ref[b, 2 * p], zero], axis=1)
                bot = jnp.concatenate([zero, s0_ref[b, 2 * p + 1]], axis=1)
                sbd_scr[b, p] = jnp.concatenate([top, bot], axis=0)

    e_mat = e_ref[...]
    row = lax.broadcasted_iota(jnp.int32, (chunk, 1), 0)
    bf = lambda x: x.astype(BF16)

    def head_sum(x):
        hi, lo = _split(x)
        return _dot(hi, e_mat) + _dot(lo, e_mat)

    def prepare(b):
        rw = rw_ref[b]
        if tb < chunk:
            rw = jnp.concatenate([rw, jnp.zeros((chunk - tb, RW_PAD), F32)], axis=0)
        prev = jnp.where(row == 0, prev_scr[b], pltpu.roll(rw, 1, axis=0))
        xs = rw + (prev - rw) * mu_ref[...]
        prev_scr[b] = rw[tb - 1:tb, :]

        r = xs[:, RW_R:RW_R + WIDTH]
        k = xs[:, RW_K:RW_K + WIDTH]
        v = xs[:, RW_V:RW_V + WIDTH]
        wd = xs[:, RW_WD:RW_WD + WD_SLOT]
        ad = xs[:, RW_AD:RW_AD + AD_SLOT]
        gd = xs[:, RW_GD:RW_GD + GD_SLOT]

        x_w = w0_ref[...] + _dot(bf(jnp.tanh(wd)), wd_ref[...])
        w_log = -_softplus(-x_w) - 0.5
        lw = -jnp.exp(w_log)
        a_sig = _sigmoid(a0_ref[...] + _dot(bf(ad), wa_ref[...]))
        g = _dot(bf(_sigmoid(gd)), wg_ref[...])
        kkr = k * kk_ref[...]
        kk = kkr / jnp.maximum(jnp.sqrt(head_sum(kkr * kkr)), 1e-12)
        kmod = k * (1.0 + (a_sig - 1.0) * ka_ref[...])
        if tb < chunk:
            valid = row < tb
            lw = jnp.where(valid, lw, 0.0)
            kk = jnp.where(valid, kk, 0.0)
            kmod = jnp.where(valid, kmod, 0.0)

        lc = lw
        shift = 1
        while shift < chunk:
            lc = lc + jnp.where(row >= shift, pltpu.roll(lc, shift, axis=0), 0.0)
            shift *= 2
        lprev = lc - lw
        lref = lc[chunk // 2 - 1:chunk // 2, :]
        lend = lc[chunk - 1:chunk, :]
        e_b = jnp.exp(lref - lc)
        e_e = jnp.exp(lend - lc)
        b_vec = kk * a_sig
        return dict(
            r=r, v=v, kmod=kmod, g=g,
            a_hat=-kk * jnp.exp(lprev - lref), a_start=-kk * jnp.exp(lprev),
            r_hat=r * jnp.exp(lc - lref), r_start=r * jnp.exp(lc),
            b_hat=b_vec * e_b, k_hat=kmod * e_b, b_til=b_vec * e_e, k_til=kmod * e_e,
            dec_end=jnp.exp(lend))

    ss = lax.broadcasted_iota(jnp.int32, (chunk, LANES), 0)
    tt = lax.broadcasted_iota(jnp.int32, (chunk, LANES), 1) % HEAD_DIM
    same_head = (lax.broadcasted_iota(jnp.int32, (LANES, LANES), 0) // HEAD_DIM
                 == lax.broadcasted_iota(jnp.int32, (LANES, LANES), 1) // HEAD_DIM)

    def stack2(x):
        parts = [jnp.where(low, x, 0.0), jnp.where(low, 0.0, x)]
        if chunk < HEAD_DIM:
            pad = jnp.zeros((HEAD_DIM - chunk, LANES), x.dtype)
            parts = [parts[0], pad, parts[1], pad]
        return jnp.concatenate(parts, axis=0)

    seqs = [prepare(b) for b in range(nb)]
    chains = [(b, p) for b in range(nb) for p in range(n_pairs)]
    part = lambda b, p, name: seqs[b][name][:, p * LANES:(p + 1) * LANES]

    sub = RWKV_SUBBLOCK
    n_sub = -(-min(chunk, tb) // sub)
    same_sub = ss // sub == tt // sub
    each = lambda f: [f(i, b, p) for i, (b, p) in enumerate(chains)]
    sbds = each(lambda i, b, p: sbd_scr[b, p])
    v2 = each(lambda i, b, p: bf(stack2(part(b, p, "v"))))
    mts = each(lambda i, b, p: _dot_nt(bf(part(b, p, "b_hat")), bf(stack2(part(b, p, "a_hat")))))
    ar = each(lambda i, b, p: bf(jnp.concatenate([part(b, p, "a_hat"), part(b, p, "r_hat")], axis=0)))
    bk = each(lambda i, b, p: bf(jnp.concatenate([stack2(part(b, p, "b_hat")), stack2(part(b, p, "k_hat"))], axis=0)))
    quad = each(lambda i, b, p: _dot_nt(ar[i], bk[i]))
    ms = [jnp.where(tt < ss, x[:chunk, :LANES], 0.0) for x in quad]
    m_aks = [jnp.where(tt < ss, x[:chunk, LANES:], 0.0) for x in quad]
    m_rbs = [jnp.where(tt <= ss, x[chunk:, :LANES], 0.0) for x in quad]
    m_rks = [jnp.where(tt <= ss, x[chunk:, LANES:], 0.0) for x in quad]
    st = each(lambda i, b, p: _dot_nt(
        bf(jnp.concatenate([part(b, p, "a_start"), part(b, p, "r_start")], axis=0)), bf(sbds[i])))
    rhss = each(lambda i, b, p: st[i][:chunk] + _dot(bf(m_aks[i]), v2[i]))
    def pack(x):
        d = jnp.where(same_sub, x, 0.0)
        out = d[0:sub]
        for blk in range(1, n_sub):
            out = out + d[blk * sub:(blk + 1) * sub]
        return out

    mtps = [pack(jnp.where(ss < tt, mt, 0.0)) for mt in mts]
    eps = [pack(m) for m in ms]
    prow = lax.broadcasted_iota(jnp.int32, (sub, 1), 0)
    group0 = lax.broadcasted_iota(jnp.int32, (sub, LANES), 1) // sub * sub
    for tau in range(1, min(sub, tb)):
        for i in range(len(chains)):
            coef = jnp.take_along_axis(mtps[i], group0 + tau, axis=1)
            contrib = jnp.sum(coef * eps[i], axis=0, keepdims=True)
            eps[i] = jnp.where(prow == tau, eps[i] + contrib, eps[i])

    def unpack(x):
        blk_of_lane = lax.broadcasted_iota(jnp.int32, (sub, LANES), 1) // sub % (HEAD_DIM // sub)
        rows = [jnp.where(blk_of_lane == blk, x, 0.0) if blk < n_sub else jnp.zeros((sub, LANES), F32)
                for blk in range(chunk // sub)]
        return jnp.concatenate(rows, axis=0)

    es = [unpack(x) for x in eps]
    size = sub
    while size < min(chunk, tb):
        coupled = ((ss // size) % 2 == 1) & (tt // size == ss // size - 1)
        ns = [jnp.where(coupled, m, 0.0) for m in ms]
        qs = each(lambda i, b, p: ns[i] + _dot3(es[i], stack2(ns[i])))
        es = each(lambda i, b, p: es[i] + qs[i] + _dot3(qs[i], stack2(es[i])))
        size *= 2
    us = each(lambda i, b, p: rhss[i] + _dot3(es[i], stack2(rhss[i])))

    ys = each(lambda i, b, p: st[i][chunk:] + _dot(bf(jnp.concatenate([m_rbs[i], m_rks[i]], axis=1)),
                                          jnp.concatenate([bf(stack2(us[i])), v2[i]], axis=0)))
    upds = each(lambda i, b, p: _dot_tn(bf(jnp.concatenate([us[i], part(b, p, "v")], axis=0)),
                                        bf(jnp.concatenate([part(b, p, "b_til"), part(b, p, "k_til")], axis=0))))
    y_parts = [[] for _ in range(nb)]
    for i, (b, p) in enumerate(chains):
        y_parts[b].append(ys[i])
        sbd_scr[b, p] = jnp.where(same_head, sbds[i] * part(b, p, "dec_end") + upds[i], 0.0)

    inv_n = 1.0 / HEAD_DIM
    for b in range(nb):
        s = seqs[b]
        y = jnp.concatenate(y_parts[b], axis=-1)
        mean = head_sum(y) * inv_n
        d = y - mean
        var = head_sum(d * d) * inv_n
        yn = d * lax.rsqrt(var + GN_EPS) * lnw_ref[...] + lnb_ref[...]
        bonus = head_sum(s["r"] * s["kmod"] * rk_ref[...]) * s["v"]
        out = (yn + bonus) * s["g"]
        o_ref[b] = out[:tb].astype(o_ref.dtype)

    @pl.when(j == pl.num_programs(1) - 1)
    def _():
        for b in range(nb):
            for p in range(n_pairs):
                sbd = sbd_scr[b, p]
                sout_ref[b, 2 * p] = sbd[:HEAD_DIM, :HEAD_DIM]
                sout_ref[b, 2 * p + 1] = sbd[HEAD_DIM:, HEAD_DIM:]


def _rwkv(rw, prev0, s0, mu, w0, a0, k_k, k_a, r_k, lnx_w, lnx_b, wd_up, wa_up, wg_up, e_mat, *, tb, nb):
    b, t, _ = rw.shape
    chunk = min(RWKV_CHUNK, -(-tb // RWKV_SUBBLOCK) * RWKV_SUBBLOCK)
    assert t % tb == 0 and tb <= chunk and b % nb == 0
    vec = _const_spec((1, WIDTH))
    state_spec = pl.BlockSpec((nb, N_HEADS, HEAD_DIM, HEAD_DIM), lambda bi, j: (bi, 0, 0, 0))
    return pl.pallas_call(
        functools.partial(_rwkv_kernel, nb=nb, tb=tb, chunk=chunk),
        grid=(b // nb, t // tb),
        in_specs=[pl.BlockSpec((nb, tb, RW_PAD), lambda bi, j: (bi, j, 0)),
                  pl.BlockSpec((nb, 1, RW_PAD), lambda bi, j: (bi, 0, 0)),
                  state_spec,
                  _const_spec((1, RW_PAD)),
                  vec, vec, vec, vec, vec, vec, vec,
                  _const_spec((WD_SLOT, WIDTH)), _const_spec((AD_SLOT, WIDTH)), _const_spec((GD_SLOT, WIDTH)),
                  _const_spec((WIDTH, WIDTH))],
        out_specs=[pl.BlockSpec((nb, tb, WIDTH), lambda bi, j: (bi, j, 0)), state_spec],
        out_shape=[jax.ShapeDtypeStruct((b, t, WIDTH), BF16),
                   jax.ShapeDtypeStruct((b, N_HEADS, HEAD_DIM, HEAD_DIM), F32)],
        scratch_shapes=[pltpu.VMEM((nb, WIDTH // LANES, LANES, LANES), F32),
                        pltpu.VMEM((nb, 1, RW_PAD), F32)],
        compiler_params=pltpu.CompilerParams(dimension_semantics=("parallel", "arbitrary"),
                                             vmem_limit_bytes=VMEM_LIMIT_BYTES),
        name="rwkv",
    )(rw, prev0, s0, mu, w0, a0, k_k, k_a, r_k, lnx_w, lnx_b, wd_up, wa_up, wg_up, e_mat)


def _merge_ffn_kernel(x_ref, osb_ref, orw_ref, gate_ref, wsb_ref, wrw_ref, wo_ref, gffn_ref, win_ref, wout_ref,
                      gfin_ref, y_ref, *, ff_chunk):
    gates = gate_ref[...]
    merged = (gates[:, :D_MODEL] * _dot(osb_ref[...], wsb_ref[...])
              + gates[:, D_MODEL:] * _dot(orw_ref[...], wrw_ref[...]))
    x = x_ref[...] + _dot(merged.astype(BF16), wo_ref[...])
    ms = jnp.mean(x * x, axis=-1, keepdims=True)
    h = (x * lax.rsqrt(ms + RMS_EPS) * gffn_ref[...]).astype(BF16)
    for c in range(D_FF // ff_chunk):
        lo = c * ff_chunk
        gate = _dot(h, win_ref[:, lo:lo + ff_chunk])
        up = _dot(h, win_ref[:, D_FF + lo:D_FF + lo + ff_chunk])
        act = (gate * _sigmoid(gate) * up).astype(BF16)
        x = x + _dot(act, wout_ref[lo:lo + ff_chunk, :])
    ms = jnp.mean(x * x, axis=-1, keepdims=True)
    y_ref[...] = x * lax.rsqrt(ms + RMS_EPS) * gfin_ref[...]


def _merge_ffn(x, o_sb, o_rw, gates, w_up_sb, w_up_rw, w_o, g_ffn, w_ffn_in, w_ffn_out, g_final, *, tm):
    n = x.shape[0]
    row = lambda w: pl.BlockSpec((tm, w), lambda i: (i, 0))
    return pl.pallas_call(
        functools.partial(_merge_ffn_kernel, ff_chunk=D_FF // 2),
        grid=(n // tm,),
        in_specs=[row(D_MODEL), row(WIDTH), row(WIDTH), row(2 * D_MODEL),
                  _const_spec((WIDTH, D_MODEL)), _const_spec((WIDTH, D_MODEL)), _const_spec((D_MODEL, D_MODEL)),
                  _const_spec((1, D_MODEL)), _const_spec((D_MODEL, 2 * D_FF)), _const_spec((D_FF, D_MODEL)),
                  _const_spec((1, D_MODEL))],
        out_specs=row(D_MODEL),
        out_shape=jax.ShapeDtypeStruct((n, D_MODEL), F32),
        compiler_params=pltpu.CompilerParams(dimension_semantics=("parallel",),
                                             vmem_limit_bytes=VMEM_LIMIT_BYTES),
        name="merge_ffn",
    )(x, o_sb, o_rw, gates, w_up_sb, w_up_rw, w_o, g_ffn, w_ffn_in, w_ffn_out, g_final)


def kernel(x_prompt, x_sample, cache_k, cache_v, page_table, state_wkv, state_shift, norm_mix_g, w_in, sb_bias, b_gate, mu_shift, w0, w_decay_up, a0, w_a_up, w_g_up, k_k, k_a, r_k, lnx_w, lnx_b, w_up_sb, w_up_rw, w_o, norm_ffn_g, w_ffn_in, w_ffn_out, norm_final_g):
    depth = w_in.shape[0]
    assert depth == 1
    bp, tp, _ = x_prompt.shape
    bs, ts, _ = x_sample.shape
    l = 0

    w = w_in[l]
    att_cols = 3 * WIDTH
    wq = w[:, 0:WIDTH].astype(BF16)
    wk = w[:, WIDTH:2 * WIDTH].astype(BF16)
    wv = w[:, 2 * WIDTH:att_cols].astype(BF16)
    wrw = _pad_rw(w[:, att_cols:att_cols + RW_COLS]).astype(BF16)
    wg = w[:, att_cols + RW_COLS:].astype(BF16)
    bg = b_gate[l][None, :]
    g_mix = norm_mix_g[l][None, :]
    mu = _pad_rw(mu_shift[l])[None, :]
    pad_rows = lambda m, n: jnp.pad(m, ((0, n - m.shape[0]), (0, 0))).astype(BF16)
    wd_up = pad_rows(w_decay_up[l], WD_SLOT)
    wa_up = pad_rows(w_a_up[l], AD_SLOT)
    wg_up = pad_rows(w_g_up[l], GD_SLOT)
    vec = lambda p: p.reshape(1, WIDTH)
    rw_vecs = (vec(w0[l]), vec(a0[l]), vec(k_k[l]), vec(k_a[l]), vec(r_k[l]), vec(lnx_w[l]), vec(lnx_b[l]))
    ch = jnp.arange(WIDTH, dtype=jnp.int32) // HEAD_DIM
    e_mat = (ch[:, None] == ch[None, :]).astype(BF16)
    ffn_w = (w_up_sb[l].astype(BF16), w_up_rw[l].astype(BF16), w_o[l].astype(BF16), norm_ffn_g[l][None, :],
             w_ffn_in[l].astype(BF16), w_ffn_out[l].astype(BF16), norm_final_g[None, :])
    bias = sb_bias[l]

    q, kt, vt, ktb, vtb, rw, gates = _inproj(x_prompt.reshape(bp * tp, D_MODEL), g_mix, wq, wk.T, wv.T, wrw, wg, bg,
                                            tm=512, seq_len=tp)
    o_sb = _sb_prompt(q.reshape(bp, tp, WIDTH), ktb, vtb, bias, tq=SB_QUERY_BLOCK, tk=SB_KEY_BLOCK,
                      pairs=SB_PAIRS_PER_STEP)
    rw3 = rw.reshape(bp, tp, RW_PAD)
    o_rw, wkv_p = _rwkv(rw3, jnp.zeros((bp, 1, RW_PAD), F32), jnp.zeros((bp, N_HEADS, HEAD_DIM, HEAD_DIM), F32),
                        mu, *rw_vecs, wd_up, wa_up, wg_up, e_mat, tb=RWKV_CHUNK, nb=RWKV_PROMPT_SEQS)
    y_p = _merge_ffn(x_prompt.reshape(bp * tp, D_MODEL), o_sb.reshape(bp * tp, WIDTH), o_rw.reshape(bp * tp, WIDTH),
                     gates, *ffn_w, tm=512)
    token_major = lambda a: jnp.transpose(a.reshape(bp, N_HEADS, HEAD_DIM, tp), (0, 3, 1, 2))[None]
    k_prompt = token_major(kt)
    v_prompt = token_major(vt)
    shift_p = _unpad_rw(rw3[:, -1, :])[None]

    q, k, v, kb, vb, rw, gates = _inproj(x_sample.reshape(bs * ts, D_MODEL), g_mix, wq, wk, wv, wrw, wg, bg, tm=512)
    sh = (bs, ts, WIDTH)
    n_phys, page = cache_k.shape[1], cache_k.shape[2]
    bias_rows = jnp.repeat(bias, ts)[:, None]
    pages_t = lambda c: jnp.transpose(c, (0, 1, 3, 4, 2)).reshape(depth * n_phys, WIDTH, page)
    o_sb = _sb_sample(q.reshape(sh), k.reshape(sh), v.reshape(sh), pages_t(cache_k), pages_t(cache_v),
                      page_table + l * n_phys, bias_rows)
    rw3 = rw.reshape(bs, ts, RW_PAD)
    o_rw, wkv_s = _rwkv(rw3, _pad_rw(state_shift[l])[:, None, :], state_wkv[l],
                        mu, *rw_vecs, wd_up, wa_up, wg_up, e_mat, tb=ts, nb=RWKV_SAMPLE_SEQS)
    y_s = _merge_ffn(x_sample.reshape(bs * ts, D_MODEL), o_sb.reshape(bs * ts, WIDTH), o_rw.reshape(bs * ts, WIDTH),
                     gates, *ffn_w, tm=512)
    k_sample = k.reshape(1, bs, ts, N_HEADS, HEAD_DIM)
    v_sample = v.reshape(1, bs, ts, N_HEADS, HEAD_DIM)
    shift_s = _unpad_rw(rw3[:, -1, :])[None]

    return (y_p.reshape(bp, tp, D_MODEL), y_s.reshape(bs, ts, D_MODEL),
            k_prompt, v_prompt, wkv_p[None], shift_p,
            k_sample, v_sample, wkv_s[None], shift_s)
```

```python
import functools

import jax
import jax.numpy as jnp
from jax import lax
from jax.experimental import pallas as pl
from jax.experimental.pallas import tpu as pltpu

F32 = jnp.float32
BF16 = jnp.bfloat16

D_MODEL = 1024
HEAD_DIM = 64
N_HEADS = 8
WIDTH = N_HEADS * HEAD_DIM
DECAY_LORA = 64
AAA_LORA = 64
GATE_LORA = 160
RW_COLS = 3 * WIDTH + DECAY_LORA + AAA_LORA + GATE_LORA
D_FF = 2816
RMS_EPS = 1e-6
GN_EPS = 64e-5
LOG2_E = 1.4426950408889634

LANES = 128
RW_R, RW_K, RW_V = 0, WIDTH, 2 * WIDTH
RW_WD, RW_AD, RW_GD = 3 * WIDTH, 3 * WIDTH + LANES, 3 * WIDTH + 2 * LANES
RW_PAD = 3 * WIDTH + 4 * LANES
WD_SLOT, AD_SLOT, GD_SLOT = LANES, LANES, 2 * LANES

VMEM_LIMIT_BYTES = 56 * 1024 * 1024

RWKV_CHUNK = 64
RWKV_SUBBLOCK = 16
RWKV_PROMPT_SEQS = 4
RWKV_SAMPLE_SEQS = 8
SB_QUERY_BLOCK = 1024
SB_KEY_BLOCK = 256
SB_PAIRS_PER_STEP = 1
SAMPLE_PAGES_PER_STEP = 32


def _dot(a, b):
    return jnp.dot(a, b, preferred_element_type=F32)


def _dot_nt(a, b):
    return lax.dot_general(a, b, (((1,), (1,)), ((), ())), preferred_element_type=F32)


def _dot_tn(a, b):
    return lax.dot_general(a, b, (((0,), (0,)), ((), ())), preferred_element_type=F32)


def _split(x):
    hi = x.astype(BF16)
    lo = (x - hi.astype(F32)).astype(BF16)
    return hi, lo


def _dot3(a_parts, b_parts):
    a_hi, a_lo = a_parts
    b_hi, b_lo = b_parts
    n = a_hi.shape[0]
    both = _dot(jnp.concatenate([a_hi, a_lo], axis=0), b_hi)
    return both[:n] + both[n:] + _dot(a_hi, b_lo)


def _softplus(z):
    return jnp.maximum(z, 0.0) + jnp.log(1.0 + jnp.exp2(jnp.abs(z) * (-LOG2_E)))


def _sigmoid(x):
    return 1.0 / (1.0 + jnp.exp(-x))


def _const_spec(shape):
    nd = len(shape)
    return pl.BlockSpec(shape, lambda *_: (0,) * nd, pipeline_mode=pl.Buffered(1))


def _pad_rw(x):
    def slot(lo, width, size):
        part = x[..., lo:lo + width]
        pad = [(0, 0)] * (x.ndim - 1) + [(0, size - width)]
        return jnp.pad(part, pad)
    o = 3 * WIDTH
    return jnp.concatenate([
        x[..., :o],
        slot(o, DECAY_LORA, WD_SLOT),
        slot(o + DECAY_LORA, AAA_LORA, AD_SLOT),
        slot(o + DECAY_LORA + AAA_LORA, GATE_LORA, GD_SLOT),
    ], axis=-1)


def _unpad_rw(x):
    return jnp.concatenate([
        x[..., :3 * WIDTH],
        x[..., RW_WD:RW_WD + DECAY_LORA],
        x[..., RW_AD:RW_AD + AAA_LORA],
        x[..., RW_GD:RW_GD + GATE_LORA],
    ], axis=-1)


def _inproj_kernel(x_ref, g_ref, wq_ref, wk_ref, wv_ref, wrw_ref, wg_ref, bg_ref,
                   q_ref, k_ref, v_ref, kb_ref, vb_ref, rw_ref, gate_ref, *, channel_major_kv):
    x = x_ref[...]
    ms = jnp.mean(x * x, axis=-1, keepdims=True)
    h = (x * lax.rsqrt(ms + RMS_EPS) * g_ref[...]).astype(BF16)
    q_ref[...] = _dot(h, wq_ref[...]) * (HEAD_DIM ** -0.5)
    proj = (lambda w_ref: _dot_nt(w_ref[...], h)) if channel_major_kv else (lambda w_ref: _dot(h, w_ref[...]))
    k = proj(wk_ref)
    k_ref[...] = k
    kb_ref[...] = k.astype(BF16)
    v = proj(wv_ref)
    v_ref[...] = v
    vb_ref[...] = v.astype(BF16)
    rw_ref[...] = _dot(h, wrw_ref[...])
    gate_ref[...] = _sigmoid(_dot(h, wg_ref[...]) + bg_ref[...])


def _inproj(x, g, wq, wk, wv, wrw, wg, bg, *, tm, seq_len=None):
    n = x.shape[0]
    row = lambda w: pl.BlockSpec((tm, w), lambda i: (i, 0))
    if seq_len is None:
        kv_spec, kv_shape, w_kv = row(WIDTH), (n, WIDTH), _const_spec((D_MODEL, WIDTH))
    else:
        assert seq_len % tm == 0
        tiles = seq_len // tm
        kv_spec = pl.BlockSpec((None, WIDTH, tm), lambda i: (i // tiles, 0, i % tiles))
        kv_shape, w_kv = (n // seq_len, WIDTH, seq_len), _const_spec((WIDTH, D_MODEL))
    return pl.pallas_call(
        functools.partial(_inproj_kernel, channel_major_kv=seq_len is not None),
        grid=(n // tm,),
        in_specs=[row(D_MODEL), _const_spec((1, D_MODEL)),
                  _const_spec((D_MODEL, WIDTH)), w_kv, w_kv,
                  _const_spec((D_MODEL, RW_PAD)), _const_spec((D_MODEL, 2 * D_MODEL)),
                  _const_spec((1, 2 * D_MODEL))],
        out_specs=[row(WIDTH), kv_spec, kv_spec, kv_spec, kv_spec, row(RW_PAD), row(2 * D_MODEL)],
        out_shape=[jax.ShapeDtypeStruct((n, WIDTH), F32),
                   jax.ShapeDtypeStruct(kv_shape, F32),
                   jax.ShapeDtypeStruct(kv_shape, F32),
                   jax.ShapeDtypeStruct(kv_shape, BF16),
                   jax.ShapeDtypeStruct(kv_shape, BF16),
                   jax.ShapeDtypeStruct((n, RW_PAD), F32),
                   jax.ShapeDtypeStruct((n, 2 * D_MODEL), F32)],
        compiler_params=pltpu.CompilerParams(dimension_semantics=("parallel",),
                                             vmem_limit_bytes=VMEM_LIMIT_BYTES),
        name="inproj",
    )(x, g, wq, wk, wv, wrw, wg, bg)


def _sb_prompt_kernel(bias_ref, q_ref, k_ref, v_ref, o_ref, acc_ref, d_scr, sp_scr, att_scr, *, tq, tk, pairs):
    group = pl.program_id(1)
    qi = pl.program_id(2)
    n_heads = 2 * pairs
    lane = lax.broadcasted_iota(jnp.int32, (1, LANES), 1)
    qh, bias = [], []
    for g in range(pairs):
        q = q_ref[:, g * LANES:(g + 1) * LANES]
        qh += [jnp.where(lane < HEAD_DIM, q, 0.0).astype(BF16), jnp.where(lane >= HEAD_DIM, q, 0.0).astype(BF16)]
        bias += [bias_ref[2 * (pairs * group + g)], bias_ref[2 * (pairs * group + g) + 1]]
    tri_r = lax.broadcasted_iota(jnp.int32, (tk, tk), 0)
    tri_c = lax.broadcasted_iota(jnp.int32, (tk, tk), 1)
    tri = (tri_r > tri_c).astype(BF16)
    r = lax.broadcasted_iota(jnp.int32, (tq, tk), 0)
    c = lax.broadcasted_iota(jnp.int32, (tq, tk), 1)
    acc_ref[...] = jnp.zeros_like(acc_ref)
    n_diag = tq // tk
    n_blocks = n_diag * (qi + 1)

    def key_block(p):
        start = pl.multiple_of(jnp.clip(n_blocks - 1 - p, 0, n_blocks - 1) * tk, tk)
        return pl.ds(start, tk)

    def scores(p, visible):
        keys = key_block(p)
        for h in range(n_heads):
            g = h // 2
            z = _dot(qh[h], k_ref[g * LANES:(g + 1) * LANES, keys]) + bias[h]
            if visible is not None:
                z = jnp.where(visible, z, -1e30)
            sp = _softplus(z)
            d_scr[h] = z - sp
            sp_scr[h] = sp.astype(BF16)

    def weights(carry):
        out = []
        for h in range(n_heads):
            sp = sp_scr[h]
            tail = _dot(sp, tri) + carry[h]
            att_scr[h] = jnp.exp(d_scr[h] - tail).astype(BF16)
            out.append(tail[:, 0:1] + sp[:, 0:1].astype(F32))
        return tuple(out)

    def values(p):
        keys = key_block(p)
        for h in range(n_heads):
            g = h // 2
            acc_ref[h] += _dot_nt(att_scr[h], v_ref[g * LANES:(g + 1) * LANES, keys])

    def trip(it, carry, visible=None):
        values(it - 2)
        carry = weights(carry)
        scores(it, visible)
        return carry

    def visible(p):
        return c + (n_diag - 1 - p) * tk < r

    scores(0, visible(0))
    carry = (jnp.zeros((tq, 1), F32),) * n_heads
    if n_diag == 1:
        att_scr[...] = jnp.zeros_like(att_scr)
    else:
        carry = weights(carry)
        scores(1, visible(1))
        for it in range(2, n_diag):
            carry = trip(it, carry, visible(it))
    def trips(i, carry):
        for k in range(n_diag):
            carry = trip(n_diag * (i + 1) + k, carry)
        return carry

    carry = lax.fori_loop(0, qi, trips, carry)
    values(n_blocks - 2)
    weights(carry)
    values(n_blocks - 1)
    for g in range(pairs):
        o_ref[:, g * LANES:(g + 1) * LANES] = jnp.where(
            lane < HEAD_DIM, acc_ref[2 * g], acc_ref[2 * g + 1]).astype(o_ref.dtype)


def _sb_prompt(q, kb, vb, bias, *, tq, tk, pairs):
    b, t, _ = q.shape
    width = pairs * LANES
    assert tq % tk == 0 and t % tq == 0 and WIDTH % width == 0
    kv_spec = pl.BlockSpec((None, width, t), lambda bi, p, i: (bi, p, 0))
    return pl.pallas_call(
        functools.partial(_sb_prompt_kernel, tq=tq, tk=tk, pairs=pairs),
        grid=(b, WIDTH // width, t // tq),
        in_specs=[pl.BlockSpec(memory_space=pltpu.SMEM),
                  pl.BlockSpec((None, tq, width), lambda bi, p, i: (bi, i, p)),
                  kv_spec, kv_spec],
        out_specs=pl.BlockSpec((None, tq, width), lambda bi, p, i: (bi, i, p)),
        out_shape=jax.ShapeDtypeStruct((b, t, WIDTH), BF16),
        scratch_shapes=[pltpu.VMEM((2 * pairs, tq, LANES), F32), pltpu.VMEM((2 * pairs, tq, tk), F32),
                        pltpu.VMEM((2 * pairs, tq, tk), BF16), pltpu.VMEM((2 * pairs, tq, tk), BF16)],
        compiler_params=pltpu.CompilerParams(dimension_semantics=("parallel", "parallel", "arbitrary"),
                                             vmem_limit_bytes=VMEM_LIMIT_BYTES),
        name="sb_prompt",
    )(bias, q, kb, vb)


def _sb_sample_kernel(pt_ref, q_ref, kn_ref, vn_ref, bias_ref, *rest, n_pages_step, t_new, page):
    kt_refs = rest[:n_pages_step]
    vt_refs = rest[n_pages_step:2 * n_pages_step]
    o_ref, qbd_ref, c_ref, acc_ref = rest[2 * n_pages_step:]
    del pt_ref
    j = pl.program_id(1)
    rows = N_HEADS * t_new
    r = lax.broadcasted_iota(jnp.int32, (page, page), 0)
    c = lax.broadcasted_iota(jnp.int32, (page, page), 1)
    tri = (r > c).astype(BF16)
    bias = bias_ref[...]

    def blocks(zs, avs):
        zs = [z + bias for z in zs]
        sps = [_softplus(z) for z in zs]
        sps_b = [sp.astype(BF16) for sp in sps]
        incl = [sp + _dot(sp_b, tri) for sp, sp_b in zip(sps, sps_b)]
        totals = [jnp.broadcast_to(t[:, 0:1], t.shape) for t in incl]
        run = c_ref[...]
        acc = acc_ref[...]
        for z, t, tot, av in zip(zs, incl, totals, avs):
            acc = acc + av(jnp.exp(z - (t + run)).astype(BF16))
            run = run + tot
        acc_ref[...] = acc
        c_ref[...] = run

    @pl.when(j == 0)
    def _():
        q = q_ref[...]
        qt = jnp.concatenate([q] * N_HEADS, axis=0)
        rr = lax.broadcasted_iota(jnp.int32, (rows, WIDTH), 0)
        cc = lax.broadcasted_iota(jnp.int32, (rows, WIDTH), 1)
        qbd_ref[...] = jnp.where(rr // t_new == cc // HEAD_DIM, qt, 0.0).astype(BF16)
        c_ref[...] = jnp.zeros_like(c_ref)
        acc_ref[...] = jnp.zeros_like(acc_ref)
        pad = jnp.zeros((page - t_new, WIDTH), F32)
        kn = jnp.concatenate([kn_ref[...], pad], axis=0).astype(BF16)
        vn = jnp.concatenate([vn_ref[...], pad], axis=0).astype(BF16)
        qrow = lax.broadcasted_iota(jnp.int32, (rows, page), 0) % t_new
        key = lax.broadcasted_iota(jnp.int32, (rows, page), 1)
        z = jnp.where(key < qrow, _dot_nt(qbd_ref[...], kn), -1e30)
        blocks([z], [lambda att: _dot(att, vn)])

    qbd = qbd_ref[...]
    blocks([_dot(qbd, kt_ref[...].astype(BF16)) for kt_ref in kt_refs],
           [lambda att, vt_ref=vt_ref: _dot_nt(att, vt_ref[...].astype(BF16)) for vt_ref in vt_refs])

    @pl.when(j == pl.num_programs(1) - 1)
    def _():
        lane_head = lax.broadcasted_iota(jnp.int32, (t_new, WIDTH), 1) // HEAD_DIM
        out = jnp.zeros((t_new, WIDTH), F32)
        for h in range(N_HEADS):
            out = out + jnp.where(lane_head == h, acc_ref[h * t_new:(h + 1) * t_new, :], 0.0)
        o_ref[...] = out.astype(o_ref.dtype)


def _sb_sample(q, k_new, v_new, cache_kt, cache_vt, page_table, bias_rows):
    s, t_new, _ = q.shape
    n_pages = page_table.shape[1]
    page = cache_kt.shape[2]
    g_pages = SAMPLE_PAGES_PER_STEP
    assert n_pages % g_pages == 0
    rows = N_HEADS * t_new

    def page_spec(g):
        return pl.BlockSpec((None, WIDTH, page),
                            lambda n, j, pt: (pt[n, n_pages - 1 - (j * g_pages + g)], 0, 0))

    seq_spec = pl.BlockSpec((None, t_new, WIDTH), lambda n, j, pt: (n, 0, 0))
    grid_spec = pltpu.PrefetchScalarGridSpec(
        num_scalar_prefetch=1,
        grid=(s, n_pages // g_pages),
        in_specs=[seq_spec, seq_spec, seq_spec,
                  pl.BlockSpec((rows, 1), lambda n, j, pt: (0, 0))]
                 + [page_spec(g) for g in range(g_pages)] * 2,
        out_specs=seq_spec,
        scratch_shapes=[pltpu.VMEM((rows, WIDTH), BF16),
                        pltpu.VMEM((rows, page), F32),
                        pltpu.VMEM((rows, WIDTH), F32)],
    )
    return pl.pallas_call(
        functools.partial(_sb_sample_kernel, n_pages_step=g_pages, t_new=t_new, page=page),
        grid_spec=grid_spec,
        out_shape=jax.ShapeDtypeStruct((s, t_new, WIDTH), BF16),
        compiler_params=pltpu.CompilerParams(dimension_semantics=("parallel", "arbitrary"),
                                             vmem_limit_bytes=VMEM_LIMIT_BYTES),
        name="sb_sample",
    )(page_table, q, k_new, v_new, bias_rows, *([cache_kt] * g_pages), *([cache_vt] * g_pages))


def _rwkv_kernel(rw_ref, prev0_ref, s0_ref, mu_ref, w0_ref, a0_ref, kk_ref, ka_ref, rk_ref, lnw_ref, lnb_ref,
                 wd_ref, wa_ref, wg_ref, e_ref,
                 o_ref, sout_ref,
                 sbd_scr, prev_scr, *, nb, tb, chunk):
    assert chunk <= HEAD_DIM and chunk % RWKV_SUBBLOCK == 0
    j = pl.program_id(1)
    n_pairs = WIDTH // LANES
    low = lax.broadcasted_iota(jnp.int32, (1, LANES), 1) < HEAD_DIM

    @pl.when(j == 0)
    def _():
        prev_scr[...] = prev0_ref[...]
        zero = jnp.zeros((HEAD_DIM, HEAD_DIM), F32)
        for b in range(nb):
            for p in range(n_pairs):
                top = jnp.concatenate([s0_ref[b, 2 * p], zero], axis=1)
                bot = jnp.concatenate([zero, s0_ref[b, 2 * p + 1]], axis=1)
                sbd_scr[b, p] = jnp.concatenate([top, bot], axis=0)

    e_mat = e_ref[...]
    row = lax.broadcasted_iota(jnp.int32, (chunk, 1), 0)
    bf = lambda x: x.astype(BF16)

    def head_sum(x):
        hi, lo = _split(x)
        return _dot(hi, e_mat) + _dot(lo, e_mat)

    def prepare(b):
        rw = rw_ref[b]
        if tb < chunk:
            rw = jnp.concatenate([rw, jnp.zeros((chunk - tb, RW_PAD), F32)], axis=0)
        prev = jnp.where(row == 0, prev_scr[b], pltpu.roll(rw, 1, axis=0))
        xs = rw + (prev - rw) * mu_ref[...]
        prev_scr[b] = rw[tb - 1:tb, :]

        r = xs[:, RW_R:RW_R + WIDTH]
        k = xs[:, RW_K:RW_K + WIDTH]
        v = xs[:, RW_V:RW_V + WIDTH]
        wd = xs[:, RW_WD:RW_WD + WD_SLOT]
        ad = xs[:, RW_AD:RW_AD + AD_SLOT]
        gd = xs[:, RW_GD:RW_GD + GD_SLOT]

        x_w = w0_ref[...] + _dot(bf(jnp.tanh(wd)), wd_ref[...])
        w_log = -_softplus(-x_w) - 0.5
        lw = -jnp.exp(w_log)
        a_sig = _sigmoid(a0_ref[...] + _dot(bf(ad), wa_ref[...]))
        g = _dot(bf(_sigmoid(gd)), wg_ref[...])
        kkr = k * kk_ref[...]
        kk = kkr / jnp.maximum(jnp.sqrt(head_sum(kkr * kkr)), 1e-12)
        kmod = k * (1.0 + (a_sig - 1.0) * ka_ref[...])
        if tb < chunk:
            valid = row < tb
            lw = jnp.where(valid, lw, 0.0)
            kk = jnp.where(valid, kk, 0.0)
            kmod = jnp.where(valid, kmod, 0.0)

        lc = lw
        shift = 1
        while shift < chunk:
            lc = lc + jnp.where(row >= shift, pltpu.roll(lc, shift, axis=0), 0.0)
            shift *= 2
        lprev = lc - lw
        lref = lc[chunk // 2 - 1:chunk // 2, :]
        lend = lc[chunk - 1:chunk, :]
        e_b = jnp.exp(lref - lc)
        e_e = jnp.exp(lend - lc)
        b_vec = kk * a_sig
        return dict(
            r=r, v=v, kmod=kmod, g=g,
            a_hat=-kk * jnp.exp(lprev - lref), a_start=-kk * jnp.exp(lprev),
            r_hat=r * jnp.exp(lc - lref), r_start=r * jnp.exp(lc),
            b_hat=b_vec * e_b, k_hat=kmod * e_b, b_til=b_vec * e_e, k_til=kmod * e_e,
            dec_end=jnp.exp(lend))

    ss = lax.broadcasted_iota(jnp.int32, (chunk, LANES), 0)
    tt = lax.broadcasted_iota(jnp.int32, (chunk, LANES), 1) % HEAD_DIM
    same_head = (lax.broadcasted_iota(jnp.int32, (LANES, LANES), 0) // HEAD_DIM
                 == lax.broadcasted_iota(jnp.int32, (LANES, LANES), 1) // HEAD_DIM)

    def stack2(x):
        zero = jnp.zeros((), x.dtype)
        parts = [jnp.where(low, x, zero), jnp.where(low, zero, x)]
        if chunk < HEAD_DIM:
            pad = jnp.zeros((HEAD_DIM - chunk, LANES), x.dtype)
            parts = [parts[0], pad, parts[1], pad]
        return jnp.concatenate(parts, axis=0)

    stack2_parts = lambda hi_lo: (stack2(hi_lo[0]), stack2(hi_lo[1]))

    seqs = [prepare(b) for b in range(nb)]
    chains = [(b, p) for b in range(nb) for p in range(n_pairs)]
    part = lambda b, p, name: seqs[b][name][:, p * LANES:(p + 1) * LANES]

    sub = RWKV_SUBBLOCK
    n_sub = -(-min(chunk, tb) // sub)
    same_sub = ss // sub == tt // sub
    each = lambda f: [f(i, b, p) for i, (b, p) in enumerate(chains)]
    sbds = each(lambda i, b, p: sbd_scr[b, p])
    v2 = each(lambda i, b, p: bf(stack2(part(b, p, "v"))))
    mts = each(lambda i, b, p: _dot_nt(bf(part(b, p, "b_hat")), bf(stack2(part(b, p, "a_hat")))))
    ar = each(lambda i, b, p: bf(jnp.concatenate([part(b, p, "a_hat"), part(b, p, "r_hat")], axis=0)))
    bk = each(lambda i, b, p: bf(jnp.concatenate([stack2(part(b, p, "b_hat")), stack2(part(b, p, "k_hat"))], axis=0)))
    quad = each(lambda i, b, p: _dot_nt(ar[i], bk[i]))
    ms = [jnp.where(tt < ss, x[:chunk, :LANES], 0.0) for x in quad]
    m_aks = [jnp.where(tt < ss, x[:chunk, LANES:], 0.0) for x in quad]
    m_rbs = [jnp.where(tt <= ss, x[chunk:, :LANES], 0.0) for x in quad]
    m_rks = [jnp.where(tt <= ss, x[chunk:, LANES:], 0.0) for x in quad]
    st = each(lambda i, b, p: _dot_nt(
        bf(jnp.concatenate([part(b, p, "a_start"), part(b, p, "r_start")], axis=0)), bf(sbds[i])))
    rhss = each(lambda i, b, p: st[i][:chunk] + _dot(bf(m_aks[i]), v2[i]))
    def pack(x):
        d = jnp.where(same_sub, x, 0.0)
        out = d[0:sub]
        for blk in range(1, n_sub):
            out = out + d[blk * sub:(blk + 1) * sub]
        return out

    mtps = [pack(jnp.where(ss < tt, mt, 0.0)) for mt in mts]
    eps = [pack(m) for m in ms]
    prow = lax.broadcasted_iota(jnp.int32, (sub, 1), 0)
    group0 = lax.broadcasted_iota(jnp.int32, (sub, LANES), 1) // sub * sub
    for tau in range(1, min(sub, tb)):
        for i in range(len(chains)):
            coef = jnp.take_along_axis(mtps[i], group0 + tau, axis=1)
            contrib = jnp.sum(coef * eps[i], axis=0, keepdims=True)
            eps[i] = jnp.where(prow == tau, eps[i] + contrib, eps[i])

    def unpack(x):
        blk_of_lane = lax.broadcasted_iota(jnp.int32, (sub, LANES), 1) // sub % (HEAD_DIM // sub)
        rows = [jnp.where(blk_of_lane == blk, x, 0.0) if blk < n_sub else jnp.zeros((sub, LANES), F32)
                for blk in range(chunk // sub)]
        return jnp.concatenate(rows, axis=0)

    es = [unpack(x) for x in eps]
    size = sub
    while size < min(chunk, tb):
        coupled = ((ss // size) % 2 == 1) & (tt // size == ss // size - 1)
        ns = [jnp.where(coupled, m, 0.0) for m in ms]
        n_parts = [_split(x) for x in ns]
        e_parts = [_split(x) for x in es]
        qs = each(lambda i, b, p: ns[i] + _dot3(e_parts[i], stack2_parts(n_parts[i])))
        q_parts = [_split(x) for x in qs]
        es = each(lambda i, b, p: es[i] + qs[i] + _dot3(q_parts[i], stack2_parts(e_parts[i])))
        size *= 2
    us = each(lambda i, b, p: rhss[i] + _dot3(_split(es[i]), stack2_parts(_split(rhss[i]))))

    ys = each(lambda i, b, p: st[i][chunk:] + _dot(bf(jnp.concatenate([m_rbs[i], m_rks[i]], axis=1)),
                                          jnp.concatenate([bf(stack2(us[i])), v2[i]], axis=0)))
    upds = each(lambda i, b, p: _dot_tn(bf(jnp.concatenate([us[i], part(b, p, "v")], axis=0)),
                                        bf(jnp.concatenate([part(b, p, "b_til"), part(b, p, "k_til")], axis=0))))
    y_parts = [[] for _ in range(nb)]
    for i, (b, p) in enumerate(chains):
        y_parts[b].append(ys[i])
        sbd_scr[b, p] = jnp.where(same_head, sbds[i] * part(b, p, "dec_end") + upds[i], 0.0)

    inv_n = 1.0 / HEAD_DIM
    for b in range(nb):
        s = seqs[b]
        y = jnp.concatenate(y_parts[b], axis=-1)
        mean = head_sum(y) * inv_n
        d = y - mean
        var = head_sum(d * d) * inv_n
        yn = d * lax.rsqrt(var + GN_EPS) * lnw_ref[...] + lnb_ref[...]
        bonus = head_sum(s["r"] * s["kmod"] * rk_ref[...]) * s["v"]
        out = (yn + bonus) * s["g"]
        o_ref[b] = out[:tb].astype(o_ref.dtype)

    @pl.when(j == pl.num_programs(1) - 1)
    def _():
        for b in range(nb):
            for p in range(n_pairs):
                sbd = sbd_scr[b, p]
                sout_ref[b, 2 * p] = sbd[:HEAD_DIM, :HEAD_DIM]
                sout_ref[b, 2 * p + 1] = sbd[HEAD_DIM:, HEAD_DIM:]


def _rwkv(rw, prev0, s0, mu, w0, a0, k_k, k_a, r_k, lnx_w, lnx_b, wd_up, wa_up, wg_up, e_mat, *, tb, nb):
    b, t, _ = rw.shape
    chunk = min(RWKV_CHUNK, -(-tb // RWKV_SUBBLOCK) * RWKV_SUBBLOCK)
    assert t % tb == 0 and tb <= chunk and b % nb == 0
    vec = _const_spec((1, WIDTH))
    state_spec = pl.BlockSpec((nb, N_HEADS, HEAD_DIM, HEAD_DIM), lambda bi, j: (bi, 0, 0, 0))
    return pl.pallas_call(
        functools.partial(_rwkv_kernel, nb=nb, tb=tb, chunk=chunk),
        grid=(b // nb, t // tb),
        in_specs=[pl.BlockSpec((nb, tb, RW_PAD), lambda bi, j: (bi, j, 0)),
                  pl.BlockSpec((nb, 1, RW_PAD), lambda bi, j: (bi, 0, 0)),
                  state_spec,
                  _const_spec((1, RW_PAD)),
                  vec, vec, vec, vec, vec, vec, vec,
                  _const_spec((WD_SLOT, WIDTH)), _const_spec((AD_SLOT, WIDTH)), _const_spec((GD_SLOT, WIDTH)),
                  _const_spec((WIDTH, WIDTH))],
        out_specs=[pl.BlockSpec((nb, tb, WIDTH), lambda bi, j: (bi, j, 0)), state_spec],
        out_shape=[jax.ShapeDtypeStruct((b, t, WIDTH), BF16),
                   jax.ShapeDtypeStruct((b, N_HEADS, HEAD_DIM, HEAD_DIM), F32)],
        scratch_shapes=[pltpu.VMEM((nb, WIDTH // LANES, LANES, LANES), F32),
                        pltpu.VMEM((nb, 1, RW_PAD), F32)],
        compiler_params=pltpu.CompilerParams(dimension_semantics=("parallel", "arbitrary"),
                                             vmem_limit_bytes=VMEM_LIMIT_BYTES),
        name="rwkv",
    )(rw, prev0, s0, mu, w0, a0, k_k, k_a, r_k, lnx_w, lnx_b, wd_up, wa_up, wg_up, e_mat)


def _merge_ffn_kernel(x_ref, osb_ref, orw_ref, gate_ref, wsb_ref, wrw_ref, wo_ref, gffn_ref, win_ref, wout_ref,
                      gfin_ref, y_ref, *, ff_chunk):
    gates = gate_ref[...]
    merged = (gates[:, :D_MODEL] * _dot(osb_ref[...], wsb_ref[...])
              + gates[:, D_MODEL:] * _dot(orw_ref[...], wrw_ref[...]))
    x = x_ref[...] + _dot(merged.astype(BF16), wo_ref[...])
    ms = jnp.mean(x * x, axis=-1, keepdims=True)
    h = (x * lax.rsqrt(ms + RMS_EPS) * gffn_ref[...]).astype(BF16)
    for c in range(D_FF // ff_chunk):
        lo = c * ff_chunk
        gate = _dot(h, win_ref[:, lo:lo + ff_chunk])
        up = _dot(h, win_ref[:, D_FF + lo:D_FF + lo + ff_chunk])
        act = (gate * _sigmoid(gate) * up).astype(BF16)
        x = x + _dot(act, wout_ref[lo:lo + ff_chunk, :])
    ms = jnp.mean(x * x, axis=-1, keepdims=True)
    y_ref[...] = x * lax.rsqrt(ms + RMS_EPS) * gfin_ref[...]


def _merge_ffn(x, o_sb, o_rw, gates, w_up_sb, w_up_rw, w_o, g_ffn, w_ffn_in, w_ffn_out, g_final, *, tm):
    n = x.shape[0]
    row = lambda w: pl.BlockSpec((tm, w), lambda i: (i, 0))
    return pl.pallas_call(
        functools.partial(_merge_ffn_kernel, ff_chunk=D_FF // 2),
        grid=(n // tm,),
        in_specs=[row(D_MODEL), row(WIDTH), row(WIDTH), row(2 * D_MODEL),
                  _const_spec((WIDTH, D_MODEL)), _const_spec((WIDTH, D_MODEL)), _const_spec((D_MODEL, D_MODEL)),
                  _const_spec((1, D_MODEL)), _const_spec((D_MODEL, 2 * D_FF)), _const_spec((D_FF, D_MODEL)),
                  _const_spec((1, D_MODEL))],
        out_specs=row(D_MODEL),
        out_shape=jax.ShapeDtypeStruct((n, D_MODEL), F32),
        compiler_params=pltpu.CompilerParams(dimension_semantics=("parallel",),
                                             vmem_limit_bytes=VMEM_LIMIT_BYTES),
        name="merge_ffn",
    )(x, o_sb, o_rw, gates, w_up_sb, w_up_rw, w_o, g_ffn, w_ffn_in, w_ffn_out, g_final)


def kernel(x_prompt, x_sample, cache_k, cache_v, page_table, state_wkv, state_shift, norm_mix_g, w_in, sb_bias, b_gate, mu_shift, w0, w_decay_up, a0, w_a_up, w_g_up, k_k, k_a, r_k, lnx_w, lnx_b, w_up_sb, w_up_rw, w_o, norm_ffn_g, w_ffn_in, w_ffn_out, norm_final_g):
    depth = w_in.shape[0]
    assert depth == 1
    bp, tp, _ = x_prompt.shape
    bs, ts, _ = x_sample.shape
    l = 0

    w = w_in[l]
    att_cols = 3 * WIDTH
    wq = w[:, 0:WIDTH].astype(BF16)
    wk = w[:, WIDTH:2 * WIDTH].astype(BF16)
    wv = w[:, 2 * WIDTH:att_cols].astype(BF16)
    wrw = _pad_rw(w[:, att_cols:att_cols + RW_COLS]).astype(BF16)
    wg = w[:, att_cols + RW_COLS:].astype(BF16)
    bg = b_gate[l][None, :]
    g_mix = norm_mix_g[l][None, :]
    mu = _pad_rw(mu_shift[l])[None, :]
    pad_rows = lambda m, n: jnp.pad(m, ((0, n - m.shape[0]), (0, 0))).astype(BF16)
    wd_up = pad_rows(w_decay_up[l], WD_SLOT)
    wa_up = pad_rows(w_a_up[l], AD_SLOT)
    wg_up = pad_rows(w_g_up[l], GD_SLOT)
    vec = lambda p: p.reshape(1, WIDTH)
    rw_vecs = (vec(w0[l]), vec(a0[l]), vec(k_k[l]), vec(k_a[l]), vec(r_k[l]), vec(lnx_w[l]), vec(lnx_b[l]))
    ch = jnp.arange(WIDTH, dtype=jnp.int32) // HEAD_DIM
    e_mat = (ch[:, None] == ch[None, :]).astype(BF16)
    ffn_w = (w_up_sb[l].astype(BF16), w_up_rw[l].astype(BF16), w_o[l].astype(BF16), norm_ffn_g[l][None, :],
             w_ffn_in[l].astype(BF16), w_ffn_out[l].astype(BF16), norm_final_g[None, :])
    bias = sb_bias[l]

    q, kt, vt, ktb, vtb, rw, gates = _inproj(x_prompt.reshape(bp * tp, D_MODEL), g_mix, wq, wk.T, wv.T, wrw, wg, bg,
                                            tm=512, seq_len=tp)
    o_sb = _sb_prompt(q.reshape(bp, tp, WIDTH), ktb, vtb, bias, tq=SB_QUERY_BLOCK, tk=SB_KEY_BLOCK,
                      pairs=SB_PAIRS_PER_STEP)
    rw3 = rw.reshape(bp, tp, RW_PAD)
    o_rw, wkv_p = _rwkv(rw3, jnp.zeros((bp, 1, RW_PAD), F32), jnp.zeros((bp, N_HEADS, HEAD_DIM, HEAD_DIM), F32),
                        mu, *rw_vecs, wd_up, wa_up, wg_up, e_mat, tb=RWKV_CHUNK, nb=RWKV_PROMPT_SEQS)
    y_p = _merge_ffn(x_prompt.reshape(bp * tp, D_MODEL), o_sb.reshape(bp * tp, WIDTH), o_rw.reshape(bp * tp, WIDTH),
                     gates, *ffn_w, tm=512)
    token_major = lambda a: jnp.transpose(a.reshape(bp, N_HEADS, HEAD_DIM, tp), (0, 3, 1, 2))[None]
    k_prompt = token_major(kt)
    v_prompt = token_major(vt)
    shift_p = _unpad_rw(rw3[:, -1, :])[None]

    q, k, v, kb, vb, rw, gates = _inproj(x_sample.reshape(bs * ts, D_MODEL), g_mix, wq, wk, wv, wrw, wg, bg, tm=512)
    sh = (bs, ts, WIDTH)
    n_phys, page = cache_k.shape[1], cache_k.shape[2]
    bias_rows = jnp.repeat(bias, ts)[:, None]
    pages_t = lambda c: jnp.transpose(c, (0, 1, 3, 4, 2)).reshape(depth * n_phys, WIDTH, page)
    o_sb = _sb_sample(q.reshape(sh), k.reshape(sh), v.reshape(sh), pages_t(cache_k), pages_t(cache_v),
                      page_table + l * n_phys, bias_rows)
    rw3 = rw.reshape(bs, ts, RW_PAD)
    o_rw, wkv_s = _rwkv(rw3, _pad_rw(state_shift[l])[:, None, :], state_wkv[l],
                        mu, *rw_vecs, wd_up, wa_up, wg_up, e_mat, tb=ts, nb=RWKV_SAMPLE_SEQS)
    y_s = _merge_ffn(x_sample.reshape(bs * ts, D_MODEL), o_sb.reshape(bs * ts, WIDTH), o_rw.reshape(bs * ts, WIDTH),
                     gates, *ffn_w, tm=512)
    k_sample = k.reshape(1, bs, ts, N_HEADS, HEAD_DIM)
    v_sample = v.reshape(1, bs, ts, N_HEADS, HEAD_DIM)
    shift_s = _unpad_rw(rw3[:, -1, :])[None]

    return (y_p.reshape(bp, tp, D_MODEL), y_s.reshape(bs, ts, D_MODEL),
            k_prompt, v_prompt, wkv_p[None], shift_p,
            k_sample, v_sample, wkv_s[None], shift_s)
```

```python
import functools

import jax
import jax.numpy as jnp
from jax import lax
from jax.experimental import pallas as pl
from jax.experimental.pallas import tpu as pltpu

F32 = jnp.float32
BF16 = jnp.bfloat16

D_MODEL = 1024
HEAD_DIM = 64
N_HEADS = 8
WIDTH = N_HEADS * HEAD_DIM
DECAY_LORA = 64
AAA_LORA = 64
GATE_LORA = 160
RW_COLS = 3 * WIDTH + DECAY_LORA + AAA_LORA + GATE_LORA
D_FF = 2816
RMS_EPS = 1e-6
GN_EPS = 64e-5
LOG2_E = 1.4426950408889634

LANES = 128
RW_R, RW_K, RW_V = 0, WIDTH, 2 * WIDTH
RW_WD, RW_AD, RW_GD = 3 * WIDTH, 3 * WIDTH + LANES, 3 * WIDTH + 2 * LANES
RW_PAD = 3 * WIDTH + 4 * LANES
WD_SLOT, AD_SLOT, GD_SLOT = LANES, LANES, 2 * LANES

VMEM_LIMIT_BYTES = 56 * 1024 * 1024

RWKV_CHUNK = 64
RWKV_SUBBLOCK = 16
RWKV_PROMPT_SEQS = 4
RWKV_SAMPLE_SEQS = 8
SB_QUERY_BLOCK = 1024
SB_KEY_BLOCK = 256
SB_PAIRS_PER_STEP = 1
SAMPLE_PAGES_PER_STEP = 32


def _dot(a, b):
    return jnp.dot(a, b, preferred_element_type=F32)


def _dot_nt(a, b):
    return lax.dot_general(a, b, (((1,), (1,)), ((), ())), preferred_element_type=F32)


def _dot_tn(a, b):
    return lax.dot_general(a, b, (((0,), (0,)), ((), ())), preferred_element_type=F32)


def _split(x):
    hi = x.astype(BF16)
    lo = (x - hi.astype(F32)).astype(BF16)
    return hi, lo


def _dot3(a_parts, b_parts):
    a_hi, a_lo = a_parts
    b_hi, b_lo = b_parts
    n = a_hi.shape[0]
    both = _dot(jnp.concatenate([a_hi, a_lo], axis=0), b_hi)
    return both[:n] + both[n:] + _dot(a_hi, b_lo)


def _softplus(z):
    return jnp.maximum(z, 0.0) + jnp.log(1.0 + jnp.exp2(jnp.abs(z) * (-LOG2_E)))


def _sigmoid(x):
    return 1.0 / (1.0 + jnp.exp(-x))


def _const_spec(shape):
    nd = len(shape)
    return pl.BlockSpec(shape, lambda *_: (0,) * nd, pipeline_mode=pl.Buffered(1))


def _pad_rw(x):
    def slot(lo, width, size):
        part = x[..., lo:lo + width]
        pad = [(0, 0)] * (x.ndim - 1) + [(0, size - width)]
        return jnp.pad(part, pad)
    o = 3 * WIDTH
    return jnp.concatenate([
        x[..., :o],
        slot(o, DECAY_LORA, WD_SLOT),
        slot(o + DECAY_LORA, AAA_LORA, AD_SLOT),
        slot(o + DECAY_LORA + AAA_LORA, GATE_LORA, GD_SLOT),
    ], axis=-1)


def _unpad_rw(x):
    return jnp.concatenate([
        x[..., :3 * WIDTH],
        x[..., RW_WD:RW_WD + DECAY_LORA],
        x[..., RW_AD:RW_AD + AAA_LORA],
        x[..., RW_GD:RW_GD + GATE_LORA],
    ], axis=-1)


def _inproj_kernel(x_ref, g_ref, wq_ref, wk_ref, wv_ref, wrw_ref, wg_ref, bg_ref,
                   q_ref, k_ref, v_ref, kb_ref, vb_ref, rw_ref, gate_ref, *, channel_major_kv):
    x = x_ref[...]
    ms = jnp.mean(x * x, axis=-1, keepdims=True)
    h = (x * lax.rsqrt(ms + RMS_EPS) * g_ref[...]).astype(BF16)
    q_ref[...] = _dot(h, wq_ref[...]) * (HEAD_DIM ** -0.5)
    proj = (lambda w_ref: _dot_nt(w_ref[...], h)) if channel_major_kv else (lambda w_ref: _dot(h, w_ref[...]))
    k = proj(wk_ref)
    k_ref[...] = k
    kb_ref[...] = k.astype(BF16)
    v = proj(wv_ref)
    v_ref[...] = v
    vb_ref[...] = v.astype(BF16)
    rw_ref[...] = _dot(h, wrw_ref[...])
    gate_ref[...] = _sigmoid(_dot(h, wg_ref[...]) + bg_ref[...])


def _inproj(x, g, wq, wk, wv, wrw, wg, bg, *, tm, seq_len=None):
    n = x.shape[0]
    row = lambda w: pl.BlockSpec((tm, w), lambda i: (i, 0))
    if seq_len is None:
        kv_spec, kv_shape, w_kv = row(WIDTH), (n, WIDTH), _const_spec((D_MODEL, WIDTH))
    else:
        assert seq_len % tm == 0
        tiles = seq_len // tm
        kv_spec = pl.BlockSpec((None, WIDTH, tm), lambda i: (i // tiles, 0, i % tiles))
        kv_shape, w_kv = (n // seq_len, WIDTH, seq_len), _const_spec((WIDTH, D_MODEL))
    return pl.pallas_call(
        functools.partial(_inproj_kernel, channel_major_kv=seq_len is not None),
        grid=(n // tm,),
        in_specs=[row(D_MODEL), _const_spec((1, D_MODEL)),
                  _const_spec((D_MODEL, WIDTH)), w_kv, w_kv,
                  _const_spec((D_MODEL, RW_PAD)), _const_spec((D_MODEL, 2 * D_MODEL)),
                  _const_spec((1, 2 * D_MODEL))],
        out_specs=[row(WIDTH), kv_spec, kv_spec, kv_spec, kv_spec, row(RW_PAD), row(2 * D_MODEL)],
        out_shape=[jax.ShapeDtypeStruct((n, WIDTH), F32),
                   jax.ShapeDtypeStruct(kv_shape, F32),
                   jax.ShapeDtypeStruct(kv_shape, F32),
                   jax.ShapeDtypeStruct(kv_shape, BF16),
                   jax.ShapeDtypeStruct(kv_shape, BF16),
                   jax.ShapeDtypeStruct((n, RW_PAD), F32),
                   jax.ShapeDtypeStruct((n, 2 * D_MODEL), F32)],
        compiler_params=pltpu.CompilerParams(dimension_semantics=("parallel",),
                                             vmem_limit_bytes=VMEM_LIMIT_BYTES),
        name="inproj",
    )(x, g, wq, wk, wv, wrw, wg, bg)


def _sb_prompt_kernel(bias_ref, q_ref, k_ref, v_ref, o_ref, acc_ref, d_scr, sp_scr, att_scr, *, tq, tk, pairs):
    group = pl.program_id(1)
    qi = pl.program_id(2)
    n_heads = 2 * pairs
    lane = lax.broadcasted_iota(jnp.int32, (1, LANES), 1)
    qh, bias = [], []
    for g in range(pairs):
        q = q_ref[:, g * LANES:(g + 1) * LANES]
        qh += [jnp.where(lane < HEAD_DIM, q, 0.0).astype(BF16), jnp.where(lane >= HEAD_DIM, q, 0.0).astype(BF16)]
        bias += [bias_ref[2 * (pairs * group + g)], bias_ref[2 * (pairs * group + g) + 1]]
    tri_r = lax.broadcasted_iota(jnp.int32, (tk, tk), 0)
    tri_c = lax.broadcasted_iota(jnp.int32, (tk, tk), 1)
    tri = (tri_r > tri_c).astype(BF16)
    r = lax.broadcasted_iota(jnp.int32, (tq, tk), 0)
    c = lax.broadcasted_iota(jnp.int32, (tq, tk), 1)
    acc_ref[...] = jnp.zeros_like(acc_ref)
    n_diag = tq // tk
    n_blocks = n_diag * (qi + 1)

    def key_block(p):
        start = pl.multiple_of(jnp.clip(n_blocks - 1 - p, 0, n_blocks - 1) * tk, tk)
        return pl.ds(start, tk)

    def scores(p, visible, first=0):
        keys = key_block(p)
        for h in range(n_heads):
            g = h // 2
            z = _dot(qh[h][first:], k_ref[g * LANES:(g + 1) * LANES, keys]) + bias[h]
            if visible is not None:
                z = jnp.where(visible[first:], z, -1e30)
            sp = _softplus(z)
            d_scr[h, first:] = z - sp
            sp_scr[h, first:] = sp.astype(BF16)

    def weights(carry, first=0):
        out = []
        for h in range(n_heads):
            sp = sp_scr[h, first:]
            tail = _dot(sp, tri) + carry[h][first:]
            att_scr[h, first:] = jnp.exp(d_scr[h, first:] - tail).astype(BF16)
            total = tail[:, 0:1] + sp[:, 0:1].astype(F32)
            if first:
                att_scr[h, :first] = jnp.zeros((first, tk), BF16)
                total = jnp.concatenate([carry[h][:first], total], axis=0)
            out.append(total)
        return tuple(out)

    def values(p):
        keys = key_block(p)
        for h in range(n_heads):
            g = h // 2
            acc_ref[h] += _dot_nt(att_scr[h], v_ref[g * LANES:(g + 1) * LANES, keys])

    def trip(it, carry, visible=None, first_weights=0, first_scores=0):
        values(it - 2)
        carry = weights(carry, first_weights)
        scores(it, visible, first_scores)
        return carry

    def first_row(p):
        return max(n_diag - 1 - p, 0) * tk

    def visible(p):
        return c + first_row(p) < r

    scores(0, visible(0), first_row(0))
    carry = (jnp.zeros((tq, 1), F32),) * n_heads
    if n_diag == 1:
        att_scr[...] = jnp.zeros_like(att_scr)
    else:
        carry = weights(carry, first_row(0))
        scores(1, visible(1), first_row(1))
        for it in range(2, n_diag):
            carry = trip(it, carry, visible(it), first_row(it - 1), first_row(it))
    def trips(i, carry):
        for k in range(n_diag):
            carry = trip(n_diag * (i + 1) + k, carry)
        return carry

    carry = lax.fori_loop(0, qi, trips, carry)
    values(n_blocks - 2)
    weights(carry)
    values(n_blocks - 1)
    for g in range(pairs):
        o_ref[:, g * LANES:(g + 1) * LANES] = jnp.where(
            lane < HEAD_DIM, acc_ref[2 * g], acc_ref[2 * g + 1]).astype(o_ref.dtype)


def _sb_prompt(q, kb, vb, bias, *, tq, tk, pairs):
    b, t, _ = q.shape
    width = pairs * LANES
    assert tq % tk == 0 and t % tq == 0 and WIDTH % width == 0
    kv_spec = pl.BlockSpec((None, width, t), lambda bi, p, i: (bi, p, 0))
    return pl.pallas_call(
        functools.partial(_sb_prompt_kernel, tq=tq, tk=tk, pairs=pairs),
        grid=(b, WIDTH // width, t // tq),
        in_specs=[pl.BlockSpec(memory_space=pltpu.SMEM),
                  pl.BlockSpec((None, tq, width), lambda bi, p, i: (bi, i, p)),
                  kv_spec, kv_spec],
        out_specs=pl.BlockSpec((None, tq, width), lambda bi, p, i: (bi, i, p)),
        out_shape=jax.ShapeDtypeStruct((b, t, WIDTH), BF16),
        scratch_shapes=[pltpu.VMEM((2 * pairs, tq, LANES), F32), pltpu.VMEM((2 * pairs, tq, tk), F32),
                        pltpu.VMEM((2 * pairs, tq, tk), BF16), pltpu.VMEM((2 * pairs, tq, tk), BF16)],
        compiler_params=pltpu.CompilerParams(dimension_semantics=("parallel", "parallel", "arbitrary"),
                                             vmem_limit_bytes=VMEM_LIMIT_BYTES),
        name="sb_prompt",
    )(bias, q, kb, vb)


def _sb_sample_kernel(pt_ref, q_ref, kn_ref, vn_ref, bias_ref, *rest, n_pages_step, t_new, page):
    kt_refs = rest[:n_pages_step]
    vt_refs = rest[n_pages_step:2 * n_pages_step]
    o_ref, qbd_ref, c_ref, acc_ref = rest[2 * n_pages_step:]
    del pt_ref
    j = pl.program_id(1)
    rows = N_HEADS * t_new
    r = lax.broadcasted_iota(jnp.int32, (page, page), 0)
    c = lax.broadcasted_iota(jnp.int32, (page, page), 1)
    tri = (r > c).astype(BF16)
    bias = bias_ref[...]

    def blocks(zs, avs):
        zs = [z + bias for z in zs]
        sps = [_softplus(z) for z in zs]
        sps_b = [sp.astype(BF16) for sp in sps]
        incl = [sp + _dot(sp_b, tri) for sp, sp_b in zip(sps, sps_b)]
        totals = [jnp.broadcast_to(t[:, 0:1], t.shape) for t in incl]
        run = c_ref[...]
        acc = acc_ref[...]
        for z, t, tot, av in zip(zs, incl, totals, avs):
            acc = acc + av(jnp.exp(z - (t + run)).astype(BF16))
            run = run + tot
        acc_ref[...] = acc
        c_ref[...] = run

    @pl.when(j == 0)
    def _():
        q = q_ref[...]
        qt = jnp.concatenate([q] * N_HEADS, axis=0)
        rr = lax.broadcasted_iota(jnp.int32, (rows, WIDTH), 0)
        cc = lax.broadcasted_iota(jnp.int32, (rows, WIDTH), 1)
        qbd_ref[...] = jnp.where(rr // t_new == cc // HEAD_DIM, qt, 0.0).astype(BF16)
        c_ref[...] = jnp.zeros_like(c_ref)
        acc_ref[...] = jnp.zeros_like(acc_ref)
        pad = jnp.zeros((page - t_new, WIDTH), F32)
        kn = jnp.concatenate([kn_ref[...], pad], axis=0).astype(BF16)
        vn = jnp.concatenate([vn_ref[...], pad], axis=0).astype(BF16)
        qrow = lax.broadcasted_iota(jnp.int32, (rows, page), 0) % t_new
        key = lax.broadcasted_iota(jnp.int32, (rows, page), 1)
        z = jnp.where(key < qrow, _dot_nt(qbd_ref[...], kn), -1e30)
        blocks([z], [lambda att: _dot(att, vn)])

    qbd = qbd_ref[...]
    blocks([_dot(qbd, kt_ref[...].astype(BF16)) for kt_ref in kt_refs],
           [lambda att, vt_ref=vt_ref: _dot_nt(att, vt_ref[...].astype(BF16)) for vt_ref in vt_refs])

    @pl.when(j == pl.num_programs(1) - 1)
    def _():
        lane_head = lax.broadcasted_iota(jnp.int32, (t_new, WIDTH), 1) // HEAD_DIM
        out = jnp.zeros((t_new, WIDTH), F32)
        for h in range(N_HEADS):
            out = out + jnp.where(lane_head == h, acc_ref[h * t_new:(h + 1) * t_new, :], 0.0)
        o_ref[...] = out.astype(o_ref.dtype)


def _sb_sample(q, k_new, v_new, cache_kt, cache_vt, page_table, bias_rows):
    s, t_new, _ = q.shape
    n_pages = page_table.shape[1]
    page = cache_kt.shape[2]
    g_pages = SAMPLE_PAGES_PER_STEP
    assert n_pages % g_pages == 0
    rows = N_HEADS * t_new

    def page_spec(g):
        return pl.BlockSpec((None, WIDTH, page),
                            lambda n, j, pt: (pt[n, n_pages - 1 - (j * g_pages + g)], 0, 0))

    seq_spec = pl.BlockSpec((None, t_new, WIDTH), lambda n, j, pt: (n, 0, 0))
    grid_spec = pltpu.PrefetchScalarGridSpec(
        num_scalar_prefetch=1,
        grid=(s, n_pages // g_pages),
        in_specs=[seq_spec, seq_spec, seq_spec,
                  pl.BlockSpec((rows, 1), lambda n, j, pt: (0, 0))]
                 + [page_spec(g) for g in range(g_pages)] * 2,
        out_specs=seq_spec,
        scratch_shapes=[pltpu.VMEM((rows, WIDTH), BF16),
                        pltpu.VMEM((rows, page), F32),
                        pltpu.VMEM((rows, WIDTH), F32)],
    )
    return pl.pallas_call(
        functools.partial(_sb_sample_kernel, n_pages_step=g_pages, t_new=t_new, page=page),
        grid_spec=grid_spec,
        out_shape=jax.ShapeDtypeStruct((s, t_new, WIDTH), BF16),
        compiler_params=pltpu.CompilerParams(dimension_semantics=("parallel", "arbitrary"),
                                             vmem_limit_bytes=VMEM_LIMIT_BYTES),
        name="sb_sample",
    )(page_table, q, k_new, v_new, bias_rows, *([cache_kt] * g_pages), *([cache_vt] * g_pages))


def _rwkv_kernel(rw_ref, prev0_ref, s0_ref, mu_ref, w0_ref, a0_ref, kk_ref, ka_ref, rk_ref, lnw_ref, lnb_ref,
                 wd_ref, wa_ref, wg_ref, e_ref,
                 o_ref, sout_ref,
                 sbd_scr, prev_scr, *, nb, tb, chunk):
    assert chunk <= HEAD_DIM and chunk % RWKV_SUBBLOCK == 0
    j = pl.program_id(1)
    n_pairs = WIDTH // LANES
    low = lax.broadcasted_iota(jnp.int32, (1, LANES), 1) < HEAD_DIM

    @pl.when(j == 0)
    def _():
        prev_scr[...] = prev0_ref[...]
        zero = jnp.zeros((HEAD_DIM, HEAD_DIM), F32)
        for b in range(nb):
            for p in range(n_pairs):
                top = jnp.concatenate([s0_ref[b, 2 * p], zero], axis=1)
                bot = jnp.concatenate([zero, s0_ref[b, 2 * p + 1]], axis=1)
                sbd_scr[b, p] = jnp.concatenate([top, bot], axis=0)

    e_mat = e_ref[...]
    row = lax.broadcasted_iota(jnp.int32, (chunk, 1), 0)
    bf = lambda x: x.astype(BF16)

    def head_sum(x):
        hi, lo = _split(x)
        return _dot(hi, e_mat) + _dot(lo, e_mat)

    def prepare(b):
        rw = rw_ref[b]
        if tb < chunk:
            rw = jnp.concatenate([rw, jnp.zeros((chunk - tb, RW_PAD), F32)], axis=0)
        prev = jnp.where(row == 0, prev_scr[b], pltpu.roll(rw, 1, axis=0))
        xs = rw + (prev - rw) * mu_ref[...]
        prev_scr[b] = rw[tb - 1:tb, :]

        r = xs[:, RW_R:RW_R + WIDTH]
        k = xs[:, RW_K:RW_K + WIDTH]
        v = xs[:, RW_V:RW_V + WIDTH]
        wd = xs[:, RW_WD:RW_WD + WD_SLOT]
        ad = xs[:, RW_AD:RW_AD + AD_SLOT]
        gd = xs[:, RW_GD:RW_GD + GD_SLOT]

        x_w = w0_ref[...] + _dot(bf(jnp.tanh(wd)), wd_ref[...])
        w_log = -_softplus(-x_w) - 0.5
        lw = -jnp.exp(w_log)
        a_sig = _sigmoid(a0_ref[...] + _dot(bf(ad), wa_ref[...]))
        g = _dot(bf(_sigmoid(gd)), wg_ref[...])
        kkr = k * kk_ref[...]
        kk = kkr / jnp.maximum(jnp.sqrt(head_sum(kkr * kkr)), 1e-12)
        kmod = k * (1.0 + (a_sig - 1.0) * ka_ref[...])
        if tb < chunk:
            valid = row < tb
            lw = jnp.where(valid, lw, 0.0)
            kk = jnp.where(valid, kk, 0.0)
            kmod = jnp.where(valid, kmod, 0.0)

        lc = lw
        shift = 1
        while shift < chunk:
            lc = lc + jnp.where(row >= shift, pltpu.roll(lc, shift, axis=0), 0.0)
            shift *= 2
        lprev = lc - lw
        lref = lc[chunk // 2 - 1:chunk // 2, :]
        lend = lc[chunk - 1:chunk, :]
        e_b = jnp.exp(lref - lc)
        e_e = jnp.exp(lend - lc)
        b_vec = kk * a_sig
        return dict(
            r=r, v=v, kmod=kmod, g=g,
            a_hat=-kk * jnp.exp(lprev - lref), a_start=-kk * jnp.exp(lprev),
            r_hat=r * jnp.exp(lc - lref), r_start=r * jnp.exp(lc),
            b_hat=b_vec * e_b, k_hat=kmod * e_b, b_til=b_vec * e_e, k_til=kmod * e_e,
            dec_end=jnp.exp(lend))

    ss = lax.broadcasted_iota(jnp.int32, (chunk, LANES), 0)
    tt = lax.broadcasted_iota(jnp.int32, (chunk, LANES), 1) % HEAD_DIM
    same_head = (lax.broadcasted_iota(jnp.int32, (LANES, LANES), 0) // HEAD_DIM
                 == lax.broadcasted_iota(jnp.int32, (LANES, LANES), 1) // HEAD_DIM)

    def stack2(x):
        zero = jnp.zeros((), x.dtype)
        parts = [jnp.where(low, x, zero), jnp.where(low, zero, x)]
        if chunk < HEAD_DIM:
            pad = jnp.zeros((HEAD_DIM - chunk, LANES), x.dtype)
            parts = [parts[0], pad, parts[1], pad]
        return jnp.concatenate(parts, axis=0)

    stack2_parts = lambda hi_lo: (stack2(hi_lo[0]), stack2(hi_lo[1]))

    seqs = [prepare(b) for b in range(nb)]
    chains = [(b, p) for b in range(nb) for p in range(n_pairs)]
    part = lambda b, p, name: seqs[b][name][:, p * LANES:(p + 1) * LANES]

    sub = RWKV_SUBBLOCK
    n_sub = -(-min(chunk, tb) // sub)
    same_sub = ss // sub == tt // sub
    each = lambda f: [f(i, b, p) for i, (b, p) in enumerate(chains)]
    sbds = each(lambda i, b, p: sbd_scr[b, p])
    v2 = each(lambda i, b, p: bf(stack2(part(b, p, "v"))))
    mts = each(lambda i, b, p: _dot_nt(bf(part(b, p, "b_hat")), bf(stack2(part(b, p, "a_hat")))))
    ar = each(lambda i, b, p: bf(jnp.concatenate([part(b, p, "a_hat"), part(b, p, "r_hat")], axis=0)))
    bk = each(lambda i, b, p: bf(jnp.concatenate([stack2(part(b, p, "b_hat")), stack2(part(b, p, "k_hat"))], axis=0)))
    quad = each(lambda i, b, p: _dot_nt(ar[i], bk[i]))
    ms = [jnp.where(tt < ss, x[:chunk, :LANES], 0.0) for x in quad]
    m_aks = [jnp.where(tt < ss, x[:chunk, LANES:], 0.0) for x in quad]
    m_rbs = [jnp.where(tt <= ss, x[chunk:, :LANES], 0.0) for x in quad]
    m_rks = [jnp.where(tt <= ss, x[chunk:, LANES:], 0.0) for x in quad]
    st = each(lambda i, b, p: _dot_nt(
        bf(jnp.concatenate([part(b, p, "a_start"), part(b, p, "r_start")], axis=0)), bf(sbds[i])))
    rhss = each(lambda i, b, p: st[i][:chunk] + _dot(bf(m_aks[i]), v2[i]))
    def pack(x):
        d = jnp.where(same_sub, x, 0.0)
        out = d[0:sub]
        for blk in range(1, n_sub):
            out = out + d[blk * sub:(blk + 1) * sub]
        return out

    mtps = [pack(jnp.where(ss < tt, mt, 0.0)) for mt in mts]
    eps = [pack(m) for m in ms]
    prow = lax.broadcasted_iota(jnp.int32, (sub, 1), 0)
    group0 = lax.broadcasted_iota(jnp.int32, (sub, LANES), 1) // sub * sub
    for tau in range(1, min(sub, tb)):
        for i in range(len(chains)):
            coef = jnp.take_along_axis(mtps[i], group0 + tau, axis=1)
            contrib = jnp.sum(coef * eps[i], axis=0, keepdims=True)
            eps[i] = jnp.where(prow == tau, eps[i] + contrib, eps[i])

    def unpack(x):
        blk_of_lane = lax.broadcasted_iota(jnp.int32, (sub, LANES), 1) // sub % (HEAD_DIM // sub)
        rows = [jnp.where(blk_of_lane == blk, x, 0.0) if blk < n_sub else jnp.zeros((sub, LANES), F32)
                for blk in range(chunk // sub)]
        return jnp.concatenate(rows, axis=0)

    es = [unpack(x) for x in eps]
    size = sub
    while size < min(chunk, tb):
        coupled = ((ss // size) % 2 == 1) & (tt // size == ss // size - 1)
        ns = [jnp.where(coupled, m, 0.0) for m in ms]
        n_parts = [_split(x) for x in ns]
        e_parts = [_split(x) for x in es]
        qs = each(lambda i, b, p: ns[i] + _dot3(e_parts[i], stack2_parts(n_parts[i])))
        q_parts = [_split(x) for x in qs]
        es = each(lambda i, b, p: es[i] + qs[i] + _dot3(q_parts[i], stack2_parts(e_parts[i])))
        size *= 2
    us = each(lambda i, b, p: rhss[i] + _dot3(_split(es[i]), stack2_parts(_split(rhss[i]))))

    ys = each(lambda i, b, p: st[i][chunk:] + _dot(bf(jnp.concatenate([m_rbs[i], m_rks[i]], axis=1)),
                                          jnp.concatenate([bf(stack2(us[i])), v2[i]], axis=0)))
    upds = each(lambda i, b, p: _dot_tn(bf(jnp.concatenate([us[i], part(b, p, "v")], axis=0)),
                                        bf(jnp.concatenate([part(b, p, "b_til"), part(b, p, "k_til")], axis=0))))
    y_parts = [[] for _ in range(nb)]
    for i, (b, p) in enumerate(chains):
        y_parts[b].append(ys[i])
        sbd_scr[b, p] = jnp.where(same_head, sbds[i] * part(b, p, "dec_end") + upds[i], 0.0)

    inv_n = 1.0 / HEAD_DIM
    for b in range(nb):
        s = seqs[b]
        y = jnp.concatenate(y_parts[b], axis=-1)
        mean = head_sum(y) * inv_n
        d = y - mean
        var = head_sum(d * d) * inv_n
        yn = d * lax.rsqrt(var + GN_EPS) * lnw_ref[...] + lnb_ref[...]
        bonus = head_sum(s["r"] * s["kmod"] * rk_ref[...]) * s["v"]
        out = (yn + bonus) * s["g"]
        o_ref[b] = out[:tb].astype(o_ref.dtype)

    @pl.when(j == pl.num_programs(1) - 1)
    def _():
        for b in range(nb):
            for p in range(n_pairs):
                sbd = sbd_scr[b, p]
                sout_ref[b, 2 * p] = sbd[:HEAD_DIM, :HEAD_DIM]
                sout_ref[b, 2 * p + 1] = sbd[HEAD_DIM:, HEAD_DIM:]


def _rwkv(rw, prev0, s0, mu, w0, a0, k_k, k_a, r_k, lnx_w, lnx_b, wd_up, wa_up, wg_up, e_mat, *, tb, nb):
    b, t, _ = rw.shape
    chunk = min(RWKV_CHUNK, -(-tb // RWKV_SUBBLOCK) * RWKV_SUBBLOCK)
    assert t % tb == 0 and tb <= chunk and b % nb == 0
    vec = _const_spec((1, WIDTH))
    state_spec = pl.BlockSpec((nb, N_HEADS, HEAD_DIM, HEAD_DIM), lambda bi, j: (bi, 0, 0, 0))
    return pl.pallas_call(
        functools.partial(_rwkv_kernel, nb=nb, tb=tb, chunk=chunk),
        grid=(b // nb, t // tb),
        in_specs=[pl.BlockSpec((nb, tb, RW_PAD), lambda bi, j: (bi, j, 0)),
                  pl.BlockSpec((nb, 1, RW_PAD), lambda bi, j: (bi, 0, 0)),
                  state_spec,
                  _const_spec((1, RW_PAD)),
                  vec, vec, vec, vec, vec, vec, vec,
                  _const_spec((WD_SLOT, WIDTH)), _const_spec((AD_SLOT, WIDTH)), _const_spec((GD_SLOT, WIDTH)),
                  _const_spec((WIDTH, WIDTH))],
        out_specs=[pl.BlockSpec((nb, tb, WIDTH), lambda bi, j: (bi, j, 0)), state_spec],
        out_shape=[jax.ShapeDtypeStruct((b, t, WIDTH), BF16),
                   jax.ShapeDtypeStruct((b, N_HEADS, HEAD_DIM, HEAD_DIM), F32)],
        scratch_shapes=[pltpu.VMEM((nb, WIDTH // LANES, LANES, LANES), F32),
                        pltpu.VMEM((nb, 1, RW_PAD), F32)],
        compiler_params=pltpu.CompilerParams(dimension_semantics=("parallel", "arbitrary"),
                                             vmem_limit_bytes=VMEM_LIMIT_BYTES),
        name="rwkv",
    )(rw, prev0, s0, mu, w0, a0, k_k, k_a, r_k, lnx_w, lnx_b, wd_up, wa_up, wg_up, e_mat)


def _merge_ffn_kernel(x_ref, osb_ref, orw_ref, gate_ref, wsb_ref, wrw_ref, wo_ref, gffn_ref, win_ref, wout_ref,
                      gfin_ref, y_ref, *, ff_chunk):
    gates = gate_ref[...]
    merged = (gates[:, :D_MODEL] * _dot(osb_ref[...], wsb_ref[...])
              + gates[:, D_MODEL:] * _dot(orw_ref[...], wrw_ref[...]))
    x = x_ref[...] + _dot(merged.astype(BF16), wo_ref[...])
    ms = jnp.mean(x * x, axis=-1, keepdims=True)
    h = (x * lax.rsqrt(ms + RMS_EPS) * gffn_ref[...]).astype(BF16)
    for c in range(D_FF // ff_chunk):
        lo = c * ff_chunk
        gate = _dot(h, win_ref[:, lo:lo + ff_chunk])
        up = _dot(h, win_ref[:, D_FF + lo:D_FF + lo + ff_chunk])
        act = (gate * _sigmoid(gate) * up).astype(BF16)
        x = x + _dot(act, wout_ref[lo:lo + ff_chunk, :])
    ms = jnp.mean(x * x, axis=-1, keepdims=True)
    y_ref[...] = x * lax.rsqrt(ms + RMS_EPS) * gfin_ref[...]


def _merge_ffn(x, o_sb, o_rw, gates, w_up_sb, w_up_rw, w_o, g_ffn, w_ffn_in, w_ffn_out, g_final, *, tm):
    n = x.shape[0]
    row = lambda w: pl.BlockSpec((tm, w), lambda i: (i, 0))
    return pl.pallas_call(
        functools.partial(_merge_ffn_kernel, ff_chunk=D_FF // 2),
        grid=(n // tm,),
        in_specs=[row(D_MODEL), row(WIDTH), row(WIDTH), row(2 * D_MODEL),
                  _const_spec((WIDTH, D_MODEL)), _const_spec((WIDTH, D_MODEL)), _const_spec((D_MODEL, D_MODEL)),
                  _const_spec((1, D_MODEL)), _const_spec((D_MODEL, 2 * D_FF)), _const_spec((D_FF, D_MODEL)),
                  _const_spec((1, D_MODEL))],
        out_specs=row(D_MODEL),
        out_shape=jax.ShapeDtypeStruct((n, D_MODEL), F32),
        compiler_params=pltpu.CompilerParams(dimension_semantics=("parallel",),
                                             vmem_limit_bytes=VMEM_LIMIT_BYTES),
        name="merge_ffn",
    )(x, o_sb, o_rw, gates, w_up_sb, w_up_rw, w_o, g_ffn, w_ffn_in, w_ffn_out, g_final)


def kernel(x_prompt, x_sample, cache_k, cache_v, page_table, state_wkv, state_shift, norm_mix_g, w_in, sb_bias, b_gate, mu_shift, w0, w_decay_up, a0, w_a_up, w_g_up, k_k, k_a, r_k, lnx_w, lnx_b, w_up_sb, w_up_rw, w_o, norm_ffn_g, w_ffn_in, w_ffn_out, norm_final_g):
    depth = w_in.shape[0]
    assert depth == 1
    bp, tp, _ = x_prompt.shape
    bs, ts, _ = x_sample.shape
    l = 0

    w = w_in[l]
    att_cols = 3 * WIDTH
    wq = w[:, 0:WIDTH].astype(BF16)
    wk = w[:, WIDTH:2 * WIDTH].astype(BF16)
    wv = w[:, 2 * WIDTH:att_cols].astype(BF16)
    wrw = _pad_rw(w[:, att_cols:att_cols + RW_COLS]).astype(BF16)
    wg = w[:, att_cols + RW_COLS:].astype(BF16)
    bg = b_gate[l][None, :]
    g_mix = norm_mix_g[l][None, :]
    mu = _pad_rw(mu_shift[l])[None, :]
    pad_rows = lambda m, n: jnp.pad(m, ((0, n - m.shape[0]), (0, 0))).astype(BF16)
    wd_up = pad_rows(w_decay_up[l], WD_SLOT)
    wa_up = pad_rows(w_a_up[l], AD_SLOT)
    wg_up = pad_rows(w_g_up[l], GD_SLOT)
    vec = lambda p: p.reshape(1, WIDTH)
    rw_vecs = (vec(w0[l]), vec(a0[l]), vec(k_k[l]), vec(k_a[l]), vec(r_k[l]), vec(lnx_w[l]), vec(lnx_b[l]))
    ch = jnp.arange(WIDTH, dtype=jnp.int32) // HEAD_DIM
    e_mat = (ch[:, None] == ch[None, :]).astype(BF16)
    ffn_w = (w_up_sb[l].astype(BF16), w_up_rw[l].astype(BF16), w_o[l].astype(BF16), norm_ffn_g[l][None, :],
             w_ffn_in[l].astype(BF16), w_ffn_out[l].astype(BF16), norm_final_g[None, :])
    bias = sb_bias[l]

    q, kt, vt, ktb, vtb, rw, gates = _inproj(x_prompt.reshape(bp * tp, D_MODEL), g_mix, wq, wk.T, wv.T, wrw, wg, bg,
                                            tm=512, seq_len=tp)
    o_sb = _sb_prompt(q.reshape(bp, tp, WIDTH), ktb, vtb, bias, tq=SB_QUERY_BLOCK, tk=SB_KEY_BLOCK,
                      pairs=SB_PAIRS_PER_STEP)
    rw3 = rw.reshape(bp, tp, RW_PAD)
    o_rw, wkv_p = _rwkv(rw3, jnp.zeros((bp, 1, RW_PAD), F32), jnp.zeros((bp, N_HEADS, HEAD_DIM, HEAD_DIM), F32),
                        mu, *rw_vecs, wd_up, wa_up, wg_up, e_mat, tb=RWKV_CHUNK, nb=RWKV_PROMPT_SEQS)
    y_p = _merge_ffn(x_prompt.reshape(bp * tp, D_MODEL), o_sb.reshape(bp * tp, WIDTH), o_rw.reshape(bp * tp, WIDTH),
                     gates, *ffn_w, tm=512)
    token_major = lambda a: jnp.transpose(a.reshape(bp, N_HEADS, HEAD_DIM, tp), (0, 3, 1, 2))[None]
    k_prompt = token_major(kt)
    v_prompt = token_major(vt)
    shift_p = _unpad_rw(rw3[:, -1, :])[None]

    q, k, v, kb, vb, rw, gates = _inproj(x_sample.reshape(bs * ts, D_MODEL), g_mix, wq, wk, wv, wrw, wg, bg, tm=512)
    sh = (bs, ts, WIDTH)
    n_phys, page = cache_k.shape[1], cache_k.shape[2]
    bias_rows = jnp.repeat(bias, ts)[:, None]
    pages_t = lambda c: jnp.transpose(c, (0, 1, 3, 4, 2)).reshape(depth * n_phys, WIDTH, page)
    o_sb = _sb_sample(q.reshape(sh), k.reshape(sh), v.reshape(sh), pages_t(cache_k), pages_t(cache_v),
                      page_table + l * n_phys, bias_rows)
    rw3 = rw.reshape(bs, ts, RW_PAD)
    o_rw, wkv_s = _rwkv(rw3, _pad_rw(state_shift[l])[:, None, :], state_wkv[l],
                        mu, *rw_vecs, wd_up, wa_up, wg_up, e_mat, tb=ts, nb=RWKV_SAMPLE_SEQS)
    y_s = _merge_ffn(x_sample.reshape(bs * ts, D_MODEL), o_sb.reshape(bs * ts, WIDTH), o_rw.reshape(bs * ts, WIDTH),
                     gates, *ffn_w, tm=512)
    k_sample = k.reshape(1, bs, ts, N_HEADS, HEAD_DIM)
    v_sample = v.reshape(1, bs, ts, N_HEADS, HEAD_DIM)
    shift_s = _unpad_rw(rw3[:, -1, :])[None]

    return (y_p.reshape(bp, tp, D_MODEL), y_s.reshape(bs, ts, D_MODEL),
            k_prompt, v_prompt, wkv_p[None], shift_p,
            k_sample, v_sample, wkv_s[None], shift_s)
```
